```python
import math
import jax, jax.numpy as jnp
from jax import lax
import numpy as np

D_MODEL = 4096
BATCH = 4
SEQ = 2048
DEPTH = 4
DEC_BATCH = 4
DEC_SEQ = 4096
PAST_LEN = 128

N_META = 16
NORM_EPS = 1e-6
ATT_HEADS = 32
Q_LORA = 1536
KV_LORA = 512
QK_NOPE = 128
QK_ROPE = 64
V_DIM = 128
QK_DIM = QK_NOPE + QK_ROPE
ATT_WIDTH = ATT_HEADS * V_DIM
ROPE_THETA = 10000.0
QUERY_BLOCK = 128
ATT_SCALE = 1.0 / math.sqrt(QK_DIM)
SSM_HEADDIM = 64
SSM_WIDTH = D_MODEL
SSM_HEADS = SSM_WIDTH // SSM_HEADDIM
SSM_GROUPS = 8
SSM_STATE = 128
CONV_WIDTH = 5
CONV_HALF = CONV_WIDTH // 2
CHUNK = 128
XBC_WIDTH = SSM_WIDTH + 2 * SSM_GROUPS * SSM_STATE
N_DIR = 2
MIX_WIDTH = ATT_WIDTH + SSM_WIDTH
IN_SPLITS = [Q_LORA, KV_LORA, QK_ROPE, ATT_WIDTH, SSM_WIDTH, XBC_WIDTH, N_DIR * SSM_HEADS]
IN_WIDTH = sum(IN_SPLITS)

kernel_name = "hymba_mla_ssd_bidir_encoder"


def rmsnorm(x, w):
    x32 = x.astype(jnp.float32)
    y = x32 * lax.rsqrt(jnp.mean(x32 * x32, axis=-1, keepdims=True) + NORM_EPS)
    return (y * w.astype(jnp.float32)).astype(x.dtype)


def apply_rope(x, cos, sin):
    half = x.shape[-1] // 2
    x32 = x.astype(jnp.float32)
    x1, x2 = x32[..., :half], x32[..., half:]
    out = jnp.concatenate([x1 * cos - x2 * sin, x2 * cos + x1 * sin], axis=-1)
    return out.astype(x.dtype)


def rope_tables(length):
    half = QK_ROPE // 2
    inv = 1.0 / (ROPE_THETA ** (jnp.arange(half, dtype=jnp.float32) / half))
    ang = jnp.arange(length, dtype=jnp.float32)[:, None] * inv[None, :]
    return jnp.cos(ang), jnp.sin(ang)


def block_attention(q, k, v):
    b, L, H, dq = q.shape
    nb = -(-L // QUERY_BLOCK)
    qp = jnp.pad(q, ((0, 0), (0, nb * QUERY_BLOCK - L), (0, 0), (0, 0)))
    qb = jnp.moveaxis(qp.reshape(b, nb, QUERY_BLOCK, H, dq), 1, 0)

    def one_block(qblk):
        s = jnp.einsum('bqhd,bkhd->bhqk', qblk, k).astype(jnp.float32) * ATT_SCALE
        p = jax.nn.softmax(s, axis=-1).astype(v.dtype)
        return jnp.einsum('bhqk,bkhd->bqhd', p, v)

    o = lax.map(one_block, qb)
    o = jnp.moveaxis(o, 0, 1).reshape(b, nb * QUERY_BLOCK, H, V_DIM)
    return o[:, :L]


def mla_branch(q_down, kv_down, k_rope_raw, gate, q_norm, w_q_up, kv_norm, w_kv_up, cos, sin):
    b, L, _ = q_down.shape
    cq = rmsnorm(q_down, q_norm)
    q = (cq @ w_q_up).reshape(b, L, ATT_HEADS, QK_DIM)
    ckv = rmsnorm(kv_down, kv_norm)
    kv = (ckv @ w_kv_up).reshape(b, L, ATT_HEADS, QK_NOPE + V_DIM)
    k_nope, v = kv[..., :QK_NOPE], kv[..., QK_NOPE:]
    q_rope = apply_rope(q[..., QK_NOPE:], cos[:, None, :], sin[:, None, :])
    k_rope = apply_rope(k_rope_raw, cos, sin)
    q = jnp.concatenate([q[..., :QK_NOPE], q_rope], axis=-1)
    k = jnp.concatenate([k_nope, jnp.broadcast_to(k_rope[:, :, None, :], (b, L, ATT_HEADS, QK_ROPE))], axis=-1)
    o = block_attention(q, k, v).reshape(b, L, ATT_WIDTH)
    return o * jax.nn.silu(gate)


def centred_conv(x, w, bias):
    L = x.shape[1]
    xp = jnp.pad(x, ((0, 0), (CONV_HALF, CONV_HALF), (0, 0)))
    out = bias
    for tap in range(CONV_WIDTH):
        out = out + xp[:, tap:tap + L] * w[tap]
    return out


def ssd_scan(xs, dt, a, bm, cm):
    b, Lp, h, p = xs.shape
    g, n = bm.shape[-2:]
    r = h // g
    c = Lp // CHUNK
    da = (dt * a).reshape(b, c, CHUNK, h)
    xdt = (xs * dt[..., None]).reshape(b, c, CHUNK, g, r, p)
    bm = bm.reshape(b, c, CHUNK, g, n)
    cm = cm.reshape(b, c, CHUNK, g, n)
    a_cum = jnp.cumsum(da, axis=2)
    seg = a_cum[:, :, :, None, :] - a_cum[:, :, None, :, :]
    causal = jnp.tril(jnp.ones((CHUNK, CHUNK), dtype=bool))[None, None, :, :, None]
    decay = jnp.exp(jnp.where(causal, seg, -jnp.inf)).reshape(b, c, CHUNK, CHUNK, g, r)
    cb = jnp.einsum('bclgn,bcsgn->bclsg', cm, bm)
    y_diag = jnp.einsum('bclsg,bclsgr,bcsgrp->bclgrp', cb, decay, xdt)
    decay_to_end = jnp.exp(a_cum[:, :, -1:, :] - a_cum).reshape(b, c, CHUNK, g, r)
    states = jnp.einsum('bclgn,bclgr,bclgrp->bcgrpn', bm, decay_to_end, xdt)
    chunk_decay = jnp.exp(a_cum[:, :, -1, :]).reshape(b, c, g, r)

    def step(carry, inp):
        st, dec = inp
        return carry * dec[..., None, None] + st, carry

    init = jnp.zeros((b, g, r, p, n), jnp.float32)
    _, prev = lax.scan(step, init, (jnp.moveaxis(states, 1, 0), jnp.moveaxis(chunk_decay, 1, 0)))
    prev = jnp.moveaxis(prev, 0, 1)
    y_off = jnp.einsum('bclgn,bcgrpn,bclgr->bclgrp', cm, prev, jnp.exp(a_cum).reshape(b, c, CHUNK, g, r))
    return (y_diag + y_off).reshape(b, Lp, h, p)


def ssd_branch(z, xbc_raw, dt_raw, conv_w, conv_b, dt_bias, a_log, d_skip, ssm_norm):
    b, L, _ = z.shape
    xbc = jax.nn.silu(centred_conv(xbc_raw, conv_w, conv_b))
    gn = SSM_GROUPS * SSM_STATE
    xs = xbc[..., :SSM_WIDTH].reshape(b, L, SSM_HEADS, SSM_HEADDIM).astype(jnp.float32)
    bm = xbc[..., SSM_WIDTH:SSM_WIDTH + gn].reshape(b, L, SSM_GROUPS, SSM_STATE).astype(jnp.float32)
    cm = xbc[..., SSM_WIDTH + gn:].reshape(b, L, SSM_GROUPS, SSM_STATE).astype(jnp.float32)
    dt = jax.nn.softplus(dt_raw.reshape(b, L, N_DIR, SSM_HEADS).astype(jnp.float32)
                         + dt_bias.astype(jnp.float32))
    a = -jnp.exp(a_log.astype(jnp.float32))
    pad = (-L) % CHUNK
    padf = lambda t: jnp.pad(t, ((0, 0), (pad, 0)) + ((0, 0),) * (t.ndim - 2))
    xp, bp, cp, dtp = padf(xs), padf(bm), padf(cm), padf(dt)
    flip = lambda t: jnp.flip(t, axis=1)
    y_fwd = ssd_scan(xp, dtp[:, :, 0], a[0], bp, cp)
    y_bwd = flip(ssd_scan(flip(xp), flip(dtp[:, :, 1]), a[1], flip(bp), flip(cp)))
    y = (y_fwd + y_bwd)[:, pad:] + xs * d_skip.astype(jnp.float32)[:, None]
    y = y.reshape(b, L, SSM_WIDTH) * jax.nn.silu(z.astype(jnp.float32))
    return rmsnorm(y, ssm_norm).astype(z.dtype)


def run_trunk(x, meta_tokens, norm_mix, w_in, q_norm, w_q_up, kv_norm, w_kv_up,
              conv_w, conv_b, dt_bias, a_log, d_skip, ssm_norm, w_out, final_norm):
    b = x.shape[0]
    meta = jnp.broadcast_to(meta_tokens[None].astype(x.dtype), (b, N_META, D_MODEL))
    h = jnp.concatenate([meta, x], axis=1)
    L = h.shape[1]
    cos, sin = rope_tables(L)
    split_idx = list(np.cumsum(IN_SPLITS)[:-1])
    for i in range(DEPTH):
        u = rmsnorm(h, norm_mix[i])
        proj = u @ w_in[i]
        q_down, kv_down, k_rope_raw, gate, z, xbc_raw, dt_raw = jnp.split(proj, split_idx, axis=-1)
        o_att = mla_branch(q_down, kv_down, k_rope_raw, gate, q_norm[i], w_q_up[i],
                           kv_norm[i], w_kv_up[i], cos, sin)
        o_ssm = ssd_branch(z, xbc_raw, dt_raw, conv_w[i], conv_b[i], dt_bias[i],
                           a_log[i], d_skip[i], ssm_norm[i])
        h = h + jnp.concatenate([o_att, o_ssm], axis=-1) @ w_out[i]
    return rmsnorm(h, final_norm)[:, N_META:]


def setup_inputs(seed: int = 0) -> dict:
    key = jax.random.key(seed)
    ks = jax.random.split(key, 20)
    f32 = jnp.float32
    nrm = lambda k, shape, s: jax.random.normal(k, shape, f32) * s
    gain = lambda k, shape: 1.0 + 0.02 * jax.random.normal(k, shape, f32)
    dt_init = jnp.exp(jax.random.uniform(ks[12], (DEPTH, N_DIR, SSM_HEADS), f32,
                                         math.log(1e-3), math.log(1e-1)))
    return {
        "x_prompt": jax.random.normal(ks[0], (BATCH, SEQ, D_MODEL), f32),
        "x_sample": jax.random.normal(ks[1], (DEC_BATCH, DEC_SEQ, D_MODEL), f32),
        "meta_tokens": nrm(ks[2], (N_META, D_MODEL), 1.0),
        "norm_mix": gain(ks[3], (DEPTH, D_MODEL)),
        "w_in": nrm(ks[4], (DEPTH, D_MODEL, IN_WIDTH), D_MODEL ** -0.5),
        "q_norm": gain(ks[5], (DEPTH, Q_LORA)),
        "w_q_up": nrm(ks[6], (DEPTH, Q_LORA, ATT_HEADS * QK_DIM), Q_LORA ** -0.5),
        "kv_norm": gain(ks[7], (DEPTH, KV_LORA)),
        "w_kv_up": nrm(ks[8], (DEPTH, KV_LORA, ATT_HEADS * (QK_NOPE + V_DIM)), KV_LORA ** -0.5),
        "conv_w": nrm(ks[9], (DEPTH, CONV_WIDTH, XBC_WIDTH), CONV_WIDTH ** -0.5),
        "conv_b": nrm(ks[10], (DEPTH, XBC_WIDTH), 0.02),
        "dt_bias": dt_init + jnp.log(-jnp.expm1(-dt_init)),
        "a_log": jnp.log(jax.random.uniform(ks[13], (DEPTH, N_DIR, SSM_HEADS), f32, 1.0, 16.0)),
        "d_skip": gain(ks[14], (DEPTH, SSM_HEADS)),
        "ssm_norm": gain(ks[15], (DEPTH, SSM_WIDTH)),
        "w_out": nrm(ks[16], (DEPTH, MIX_WIDTH, D_MODEL), MIX_WIDTH ** -0.5),
        "final_norm": gain(ks[17], (D_MODEL,)),
    }


def reference(x_prompt, x_sample, meta_tokens, norm_mix, w_in, q_norm, w_q_up, kv_norm, w_kv_up,
              conv_w, conv_b, dt_bias, a_log, d_skip, ssm_norm, w_out, final_norm):
    y_prompt = run_trunk(x_prompt, meta_tokens, norm_mix, w_in, q_norm, w_q_up, kv_norm, w_kv_up,
                         conv_w, conv_b, dt_bias, a_log, d_skip, ssm_norm, w_out, final_norm)
    y_sample = run_trunk(x_sample, meta_tokens, norm_mix, w_in, q_norm, w_q_up, kv_norm, w_kv_up,
                         conv_w, conv_b, dt_bias, a_log, d_skip, ssm_norm, w_out, final_norm)
    return (y_prompt, y_sample)
```

```python
import functools
import math
from typing import NamedTuple

import jax
import jax.numpy as jnp
from jax import lax
from jax.experimental import pallas as pl
from jax.experimental.pallas import tpu as pltpu

F32 = jnp.float32
BF16 = jnp.bfloat16

N_META = 16
NORM_EPS = 1e-6
QK_NOPE = 128
QK_ROPE = 64
V_DIM = 128
ROPE_THETA = 10000.0
SSM_HEADDIM = 64
SSM_STATE = 128
CONV_WIDTH = 5
CONV_HALF = CONV_WIDTH // 2
CHUNK = 128
PAD = CHUNK - N_META
GROUP_HEADS = 8
DT_LANES = 128
DT_BWD = 64
LANE = 128
SUBLANE = 8
VMEM_LIMIT = 56 * 1024 * 1024
ATT_KEY_CHUNK = 512


class Cfg(NamedTuple):
    d_model: int = 4096
    att_heads: int = 32
    q_lora: int = 1536
    kv_lora: int = 512
    ssm_heads: int = 64
    ssm_groups: int = 8

    @property
    def att_w(self):
        return self.att_heads * V_DIM

    @property
    def ssm_w(self):
        return self.ssm_heads * SSM_HEADDIM

    @property
    def gn(self):
        return self.ssm_groups * SSM_STATE

    @property
    def xbc_w(self):
        return self.ssm_w + 2 * self.gn

    @property
    def q_piece(self):
        return self.q_lora // 3

    @property
    def off_z(self):
        return 0

    @property
    def off_gate(self):
        return self.ssm_w

    @property
    def off_xbc(self):
        return self.off_gate + self.att_w

    @property
    def off_kv(self):
        return self.off_xbc + self.xbc_w

    @property
    def off_q(self):
        return self.off_kv + self.kv_lora

    @property
    def off_dt(self):
        return self.off_q + self.q_lora

    @property
    def off_kr(self):
        return self.off_dt + DT_LANES

    @property
    def proj_w(self):
        return self.off_kr + LANE

    @property
    def q_w(self):
        return (self.att_heads // 2) * 384


def _check_cfg(cfg):
    assert cfg.ssm_w == cfg.d_model == cfg.att_w
    assert cfg.ssm_heads == cfg.ssm_groups * GROUP_HEADS and cfg.ssm_heads <= DT_BWD
    assert cfg.att_heads % 2 == 0 and cfg.q_lora % 3 == 0 and cfg.q_piece % LANE == 0
    assert cfg.kv_lora % LANE == 0
    assert cfg.off_kv % cfg.kv_lora == 0 and cfg.off_q % cfg.q_piece == 0


def _pick(n, target, unit):
    best = None
    for t in range(unit, min(n, target) + 1, unit):
        if n % t == 0:
            best = t
    assert best is not None, (n, target, unit)
    return best


def _params(n_axes):
    return pltpu.CompilerParams(dimension_semantics=("arbitrary",) * n_axes,
                                vmem_limit_bytes=VMEM_LIMIT)


def _rmsnorm_rows(dst_ref, piece_refs, gain_refs, rows, row_chunk=64):
    width = sum(p.shape[1] for p in piece_refs)
    row_chunk = min(row_chunk, rows)

    def body(r, carry):
        sl = pl.ds(pl.multiple_of(r * row_chunk, row_chunk), row_chunk)
        xs = [p[sl, :] for p in piece_refs]
        ss = xs[0] * xs[0]
        ss = jnp.sum(ss, axis=-1, keepdims=True)
        for x in xs[1:]:
            ss = ss + jnp.sum(x * x, axis=-1, keepdims=True)
        rstd = lax.rsqrt(ss / width + NORM_EPS)
        off = 0
        for x, g in zip(xs, gain_refs):
            w = x.shape[1]
            dst_ref[sl, off:off + w] = (x * rstd * g[...]).astype(BF16)
            off += w
        return carry

    lax.fori_loop(0, rows // row_chunk, body, 0)


def _in_proj_kernel(x_ref, g_ref, w_ref, o_ref, u_ref):
    @pl.when(pl.program_id(1) == 0)
    def _():
        _rmsnorm_rows(u_ref, [x_ref], [g_ref], x_ref.shape[0])

    o_ref[...] = jnp.dot(u_ref[...], w_ref[...], preferred_element_type=F32)


def _in_proj(h, gain, w, cfg):
    rows, d = h.shape
    n = w.shape[1]
    tm = _pick(rows, 512, LANE)
    tn = _pick(n, 640, LANE)
    return pl.pallas_call(
        _in_proj_kernel,
        out_shape=jax.ShapeDtypeStruct((rows, n), F32),
        grid=(rows // tm, n // tn),
        in_specs=[pl.BlockSpec((tm, d), lambda i, j: (i, 0)),
                  pl.BlockSpec((1, d), lambda i, j: (0, 0)),
                  pl.BlockSpec((d, tn), lambda i, j: (0, j))],
        out_specs=pl.BlockSpec((tm, tn), lambda i, j: (i, j)),
        scratch_shapes=[pltpu.VMEM((tm, d), BF16)],
        compiler_params=_params(2),
        name="in_proj",
    )(h, gain, w)


def _rope128(r, cos_t, sin_t):
    lane = lax.broadcasted_iota(jnp.int32, (1, LANE), 1)
    first_half = (lane % QK_ROPE) < (QK_ROPE // 2)
    partner = jnp.where(first_half,
                        pltpu.roll(r, LANE - QK_ROPE // 2, 1),
                        pltpu.roll(r, QK_ROPE // 2, 1))
    return r * cos_t + partner * sin_t


def _rope_tables(batch, lp):
    half = QK_ROPE // 2
    inv = 1.0 / (ROPE_THETA ** (jnp.arange(half, dtype=F32) / half))
    pos = jnp.maximum(jnp.arange(lp, dtype=F32) - PAD, 0.0)
    ang = pos[:, None] * inv[None, :]
    cos, sin = jnp.cos(ang), jnp.sin(ang)
    cos_t = jnp.tile(cos, (batch, LANE // half))
    sin_t = jnp.tile(jnp.concatenate([-sin, sin], axis=1), (batch, LANE // QK_ROPE))
    return cos_t, sin_t


def _q_up_kernel(x0_ref, x1_ref, x2_ref, g0_ref, g1_ref, g2_ref, w_ref, cos_ref, sin_ref,
                 o_ref, u_ref):
    @pl.when(pl.program_id(1) == 0)
    def _():
        _rmsnorm_rows(u_ref, [x0_ref, x1_ref, x2_ref], [g0_ref, g1_ref, g2_ref],
                      x0_ref.shape[0])

    acc = jnp.dot(u_ref[...], w_ref[...], preferred_element_type=F32)
    pair_w = 2 * QK_NOPE + LANE
    for p in range(acc.shape[1] // pair_w):
        base = p * pair_w
        o_ref[:, base:base + 2 * QK_NOPE] = acc[:, base:base + 2 * QK_NOPE].astype(BF16)
        r = acc[:, base + 2 * QK_NOPE:base + pair_w]
        o_ref[:, base + 2 * QK_NOPE:base + pair_w] = _rope128(
            r, cos_ref[...], sin_ref[...]).astype(BF16)


def _q_up(proj, gain, w, cos_t, sin_t, cfg):
    rows = proj.shape[0]
    tm = _pick(rows, 512, LANE)
    qp = cfg.q_piece
    tn = _pick(cfg.q_w, 768, 384)
    pb = cfg.off_q // qp
    x_specs = [pl.BlockSpec((tm, qp), functools.partial(lambda i, j, k: (i, pb + k), k=k))
               for k in range(3)]
    g_specs = [pl.BlockSpec((1, qp), functools.partial(lambda i, j, k: (0, k), k=k))
               for k in range(3)]
    return pl.pallas_call(
        _q_up_kernel,
        out_shape=jax.ShapeDtypeStruct((rows, cfg.q_w), BF16),
        grid=(rows // tm, cfg.q_w // tn),
        in_specs=x_specs + g_specs + [
            pl.BlockSpec((cfg.q_lora, tn), lambda i, j: (0, j)),
            pl.BlockSpec((tm, LANE), lambda i, j: (i, 0)),
            pl.BlockSpec((tm, LANE), lambda i, j: (i, 0))],
        out_specs=pl.BlockSpec((tm, tn), lambda i, j: (i, j)),
        scratch_shapes=[pltpu.VMEM((tm, cfg.q_lora), BF16)],
        compiler_params=_params(2),
        name="q_up",
    )(proj, proj, proj, gain, gain, gain, w, cos_t, sin_t)


def _k_rope_kernel(x_ref, cos_ref, sin_ref, o_ref):
    r = _rope128(x_ref[...], cos_ref[...], sin_ref[...])
    o_ref[...] = r.T[:QK_ROPE, :].astype(BF16)


def _k_rope(proj, cos_t, sin_t, cfg):
    rows = proj.shape[0]
    tm = _pick(rows, 512, LANE)
    cb = cfg.off_kr // LANE
    return pl.pallas_call(
        _k_rope_kernel,
        out_shape=jax.ShapeDtypeStruct((QK_ROPE, rows), BF16),
        grid=(rows // tm,),
        in_specs=[pl.BlockSpec((tm, LANE), lambda i: (i, cb)),
                  pl.BlockSpec((tm, LANE), lambda i: (i, 0)),
                  pl.BlockSpec((tm, LANE), lambda i: (i, 0))],
        out_specs=pl.BlockSpec((QK_ROPE, tm), lambda i: (0, i)),
        compiler_params=_params(1),
        name="k_rope",
    )(proj, cos_t, sin_t)


def _kv_up_kernel(x_ref, g_ref, wk_ref, wv_ref, kr_ref, kt_ref, v_ref, u_ref):
    @pl.when(pl.program_id(1) == 0)
    def _():
        _rmsnorm_rows(u_ref, [x_ref], [g_ref], x_ref.shape[0])

    u = u_ref[...]
    kn = lax.dot_general(wk_ref[...], u, (((1,), (1,)), ((), ())),
                         preferred_element_type=F32)
    kr = kr_ref[...]
    zero = jnp.zeros_like(kr)
    heads = wk_ref.shape[0] // QK_NOPE
    for t in range(heads):
        base = t * 2 * QK_NOPE
        kt_ref[base:base + QK_NOPE, :] = kn[t * QK_NOPE:(t + 1) * QK_NOPE, :].astype(BF16)
        slot_a, slot_b = (kr, zero) if t % 2 == 0 else (zero, kr)
        kt_ref[base + QK_NOPE:base + QK_NOPE + QK_ROPE, :] = slot_a
        kt_ref[base + QK_NOPE + QK_ROPE:base + 2 * QK_NOPE, :] = slot_b
    v_ref[...] = jnp.dot(u, wv_ref[...], preferred_element_type=F32).astype(BF16)


def _kv_up(proj, gain, wk_t, wv, kr_t, cfg):
    rows = proj.shape[0]
    tm = _pick(rows, 512, LANE)
    hb = 8 if cfg.att_heads % 8 == 0 else 2
    kvb = cfg.off_kv // cfg.kv_lora
    return pl.pallas_call(
        _kv_up_kernel,
        out_shape=(jax.ShapeDtypeStruct((cfg.att_heads * 2 * QK_NOPE, rows), BF16),
                   jax.ShapeDtypeStruct((rows, cfg.att_w), BF16)),
        grid=(rows // tm, cfg.att_heads // hb),
        in_specs=[pl.BlockSpec((tm, cfg.kv_lora), lambda i, j: (i, kvb)),
                  pl.BlockSpec((1, cfg.kv_lora), lambda i, j: (0, 0)),
                  pl.BlockSpec((hb * QK_NOPE, cfg.kv_lora), lambda i, j: (j, 0)),
                  pl.BlockSpec((cfg.kv_lora, hb * V_DIM), lambda i, j: (0, j)),
                  pl.BlockSpec((QK_ROPE, tm), lambda i, j: (0, i))],
        out_specs=(pl.BlockSpec((hb * 2 * QK_NOPE, tm), lambda i, j: (j, i)),
                   pl.BlockSpec((tm, hb * V_DIM), lambda i, j: (i, j))),
        scratch_shapes=[pltpu.VMEM((tm, cfg.kv_lora), BF16)],
        compiler_params=_params(2),
        name="kv_up",
    )(proj, gain, wk_t, wv, kr_t)


def _attention_kernel(qn_ref, qr_ref, kt_ref, v_ref, gate_ref, o_ref, *, scale, key_chunk):
    q = jnp.concatenate([qn_ref[...], qr_ref[...]], axis=1)
    lp = kt_ref.shape[1]

    s = jnp.dot(q, kt_ref[:, 0:CHUNK], preferred_element_type=F32) * scale
    col = lax.broadcasted_iota(jnp.int32, (1, CHUNK), 1)
    s = jnp.where(col >= PAD, s, -jnp.inf)
    m = jnp.max(s, axis=1, keepdims=True)
    p = jnp.exp(s - m)
    l = jnp.sum(p, axis=1, keepdims=True)
    acc = jnp.dot(p.astype(BF16), v_ref[0:CHUNK, :], preferred_element_type=F32)

    for a in range(CHUNK, lp, key_chunk):
        s = jnp.dot(q, kt_ref[:, a:a + key_chunk], preferred_element_type=F32) * scale
        m_new = jnp.maximum(m, jnp.max(s, axis=1, keepdims=True))
        alpha = jnp.exp(m - m_new)
        p = jnp.exp(s - m_new)
        l = alpha * l + jnp.sum(p, axis=1, keepdims=True)
        acc = alpha * acc + jnp.dot(p.astype(BF16), v_ref[a:a + key_chunk, :],
                                    preferred_element_type=F32)
        m = m_new

    g = gate_ref[...]
    o_ref[...] = ((acc / l) * (g * jax.nn.sigmoid(g))).astype(BF16)


def _attention(q, kt, v, proj, batch, lp, nq, cfg):
    rows = q.shape[0]
    tq = lp // nq
    assert lp % nq == 0 and tq % 16 == 0
    key_chunk = _pick(lp - CHUNK, ATT_KEY_CHUNK, LANE)
    gate_b = cfg.off_gate // LANE
    scale = 1.0 / math.sqrt(QK_NOPE + QK_ROPE)
    kern = functools.partial(_attention_kernel, scale=scale, key_chunk=key_chunk)
    return pl.pallas_call(
        kern,
        out_shape=jax.ShapeDtypeStruct((rows, cfg.att_w), BF16),
        grid=(batch, cfg.att_heads, nq),
        in_specs=[
            pl.BlockSpec((tq, QK_NOPE), lambda b, h, i: (b * nq + i, (h // 2) * 3 + h % 2)),
            pl.BlockSpec((tq, LANE), lambda b, h, i: (b * nq + i, (h // 2) * 3 + 2)),
            pl.BlockSpec((2 * QK_NOPE, lp), lambda b, h, i: (h, b)),
            pl.BlockSpec((lp, V_DIM), lambda b, h, i: (b, h)),
            pl.BlockSpec((tq, V_DIM), lambda b, h, i: (b * nq + i, gate_b + h))],
        out_specs=pl.BlockSpec((tq, V_DIM), lambda b, h, i: (b * nq + i, h)),
        compiler_params=_params(3),
        name="attention",
    )(q, q, kt, v, proj)


def _conv_kernel(x_ref, prev_ref, next_ref, w_ref, b_ref, o_ref, ext_ref):
    i = pl.program_id(0)
    tm = x_ref.shape[0]
    ext_ref[0:SUBLANE, :] = jnp.where(i == 0, 0.0, prev_ref[...])
    ext_ref[SUBLANE:SUBLANE + tm, :] = x_ref[...]
    ext_ref[SUBLANE + tm:2 * SUBLANE + tm, :] = jnp.where(
        i == pl.num_programs(0) - 1, 0.0, next_ref[...])
    acc = b_ref[...]
    for tap in range(CONV_WIDTH):
        acc = acc + ext_ref[pl.ds(SUBLANE - CONV_HALF + tap, tm), :] * w_ref[tap:tap + 1, :]
    o_ref[...] = acc * jax.nn.sigmoid(acc)


def _conv(proj, w, b, cfg):
    rows = proj.shape[0]
    tm = _pick(rows, 512, LANE)
    tc = _pick(cfg.xbc_w, 512, LANE)
    assert cfg.off_xbc % tc == 0
    cb = cfg.off_xbc // tc
    hb = tm // SUBLANE
    last = rows // SUBLANE - 1
    return pl.pallas_call(
        _conv_kernel,
        out_shape=jax.ShapeDtypeStruct((rows, cfg.xbc_w), F32),
        grid=(rows // tm, cfg.xbc_w // tc),
        in_specs=[pl.BlockSpec((tm, tc), lambda i, j: (i, cb + j)),
                  pl.BlockSpec((SUBLANE, tc), lambda i, j: (jnp.maximum(i * hb - 1, 0), cb + j)),
                  pl.BlockSpec((SUBLANE, tc),
                               lambda i, j: (jnp.minimum((i + 1) * hb, last), cb + j)),
                  pl.BlockSpec((CONV_WIDTH, tc), lambda i, j: (0, j)),
                  pl.BlockSpec((1, tc), lambda i, j: (0, j))],
        out_specs=pl.BlockSpec((tm, tc), lambda i, j: (i, j)),
        scratch_shapes=[pltpu.VMEM((tm + 2 * SUBLANE, tc), F32)],
        compiler_params=_params(2),
        name="conv_silu",
    )(proj, proj, proj, w, b)


def _ssd_prep_kernel(raw_ref, bias_ref, alog_ref, cum_ref, cumt_ref, dtt_ref, wt_ref, etot_ref):
    x = raw_ref[...] + bias_ref[...]
    dt = jnp.maximum(x, 0.0) + jnp.log1p(jnp.exp(-jnp.abs(x)))
    row = lax.broadcasted_iota(jnp.int32, (CHUNK, DT_LANES), 0)
    lane = lax.broadcasted_iota(jnp.int32, (CHUNK, DT_LANES), 1)
    dt = jnp.where(jnp.logical_and(pl.program_id(1) == 0, row < PAD), 0.0, dt)
    da = dt * (-jnp.exp(alog_ref[...]))
    fwd = da
    bwd = da
    k = 1
    while k < CHUNK:
        fwd = fwd + jnp.where(row >= k, pltpu.roll(fwd, k, 0), 0.0)
        bwd = bwd + jnp.where(row < CHUNK - k, pltpu.roll(bwd, CHUNK - k, 0), 0.0)
        k *= 2
    is_fwd = lane < DT_BWD
    cum = jnp.where(is_fwd, fwd, bwd)
    tot = jnp.where(is_fwd[0:1, :], fwd[CHUNK - 1:CHUNK, :], bwd[0:1, :])
    w = jnp.exp(tot - cum) * dt
    cum_ref[...] = cum
    cumt_ref[...] = cum.T
    dtt_ref[...] = dt.T
    wt_ref[...] = w.T
    etot_ref[...] = jnp.broadcast_to(jnp.exp(tot), (CHUNK, DT_LANES)).T


def _ssd_prep(proj, bias, alog, batch, lp, cfg):
    rows = proj.shape[0]
    nc = lp // CHUNK
    cb = cfg.off_dt // DT_LANES
    row_major = pl.BlockSpec((CHUNK, DT_LANES), lambda b, c: (b * nc + c, 0))
    col_major = pl.BlockSpec((DT_LANES, CHUNK), lambda b, c: (0, b * nc + c))
    return pl.pallas_call(
        _ssd_prep_kernel,
        out_shape=(jax.ShapeDtypeStruct((rows, DT_LANES), F32),
                   jax.ShapeDtypeStruct((DT_LANES, rows), F32),
                   jax.ShapeDtypeStruct((DT_LANES, rows), F32),
                   jax.ShapeDtypeStruct((DT_LANES, rows), F32),
                   jax.ShapeDtypeStruct((rows, DT_LANES), F32)),
        grid=(batch, nc),
        in_specs=[pl.BlockSpec((CHUNK, DT_LANES), lambda b, c: (b * nc + c, cb)),
                  pl.BlockSpec((1, DT_LANES), lambda b, c: (0, 0)),
                  pl.BlockSpec((1, DT_LANES), lambda b, c: (0, 0))],
        out_specs=(row_major, col_major, col_major, col_major, row_major),
        compiler_params=_params(2),
        name="ssd_prep",
    )(proj, bias, alog)


def _ssd_direction(reverse, head_base, xs_ref, bm_ref, cm_ref, cum_ref, cumt_ref, dtt_ref,
                   wt_ref, etot_ref, y_ref, state_ref):
    bm = bm_ref[...]
    cm = cm_ref[...]
    bm_t = bm.T
    cb = jnp.dot(cm.astype(BF16), bm_t.astype(BF16), preferred_element_type=F32)
    row = lax.broadcasted_iota(jnp.int32, (CHUNK, CHUNK), 0)
    col = lax.broadcasted_iota(jnp.int32, (CHUNK, CHUNK), 1)
    live = (col >= row) if reverse else (row >= col)
    lane = lax.broadcasted_iota(jnp.int32, (CHUNK, DT_LANES), 1)
    low_half = lax.broadcasted_iota(jnp.int32, (1, LANE), 1) < SSM_HEADDIM
    cum = cum_ref[...]
    for pair in range(GROUP_HEADS // 2):
        cols = slice(pair * LANE, (pair + 1) * LANE)
        xs_pair = xs_ref[:, cols].astype(BF16)
        state = state_ref[pair]
        rhs = jnp.concatenate([xs_pair, state.astype(BF16)], axis=0)
        ys, sts, ets = [], [], []
        for t in range(2):
            hd = head_base + 2 * pair + t
            a_col = jnp.sum(jnp.where(lane == hd, cum, 0.0), axis=1, keepdims=True)
            a_row = cumt_ref[pl.ds(hd, 1), :]
            decay = jnp.exp(jnp.where(live, a_col - a_row, -jnp.inf))
            mix = cb * decay * dtt_ref[pl.ds(hd, 1), :]
            carry_in = cm * jnp.exp(a_col)
            lhs = jnp.concatenate([mix, carry_in], axis=1).astype(BF16)
            ys.append(jnp.dot(lhs, rhs, preferred_element_type=F32))
            weighted_b = (bm_t * wt_ref[pl.ds(hd, 1), :]).astype(BF16)
            sts.append(jnp.dot(weighted_b, xs_pair, preferred_element_type=F32))
            ets.append(etot_ref[pl.ds(hd, 1), :])
        y_ref[:, cols] = jnp.where(low_half, ys[0], ys[1])
        state_ref[pair] = (state * jnp.where(low_half, ets[0], ets[1])
                           + jnp.where(low_half, sts[0], sts[1]))


def _ssd_kernel(*refs):
    fwd_in, bwd_in = refs[0:8], refs[8:16]
    yf_ref, yb_ref, state_ref = refs[16], refs[17], refs[18]

    @pl.when(pl.program_id(2) == 0)
    def _():
        state_ref[...] = jnp.zeros_like(state_ref)

    g = pl.program_id(1)
    _ssd_direction(False, g * GROUP_HEADS, *fwd_in, yf_ref, state_ref.at[0])
    _ssd_direction(True, DT_BWD + g * GROUP_HEADS, *bwd_in, yb_ref, state_ref.at[1])


def _ssd(xbc, cum, cum_t, dt_t, w_t, etot, batch, lp, cfg):
    rows = xbc.shape[0]
    nc = lp // CHUNK
    gw = GROUP_HEADS * SSM_HEADDIM
    bm_b = cfg.ssm_w // SSM_STATE
    cm_b = bm_b + cfg.ssm_groups

    def specs(chunk_of):
        r = lambda b, c: b * nc + chunk_of(c)
        return [pl.BlockSpec((CHUNK, gw), lambda b, g, c: (r(b, c), g)),
                pl.BlockSpec((CHUNK, SSM_STATE), lambda b, g, c: (r(b, c), bm_b + g)),
                pl.BlockSpec((CHUNK, SSM_STATE), lambda b, g, c: (r(b, c), cm_b + g)),
                pl.BlockSpec((CHUNK, DT_LANES), lambda b, g, c: (r(b, c), 0)),
                pl.BlockSpec((DT_LANES, CHUNK), lambda b, g, c: (0, r(b, c))),
                pl.BlockSpec((DT_LANES, CHUNK), lambda b, g, c: (0, r(b, c))),
                pl.BlockSpec((DT_LANES, CHUNK), lambda b, g, c: (0, r(b, c))),
                pl.BlockSpec((CHUNK, DT_LANES), lambda b, g, c: (r(b, c), 0))]

    fwd_chunk = lambda c: c
    bwd_chunk = lambda c: nc - 1 - c
    args = (xbc, xbc, xbc, cum, cum_t, dt_t, w_t, etot)
    return pl.pallas_call(
        _ssd_kernel,
        out_shape=(jax.ShapeDtypeStruct((rows, cfg.ssm_w), F32),
                   jax.ShapeDtypeStruct((rows, cfg.ssm_w), F32)),
        grid=(batch, cfg.ssm_groups, nc),
        in_specs=specs(fwd_chunk) + specs(bwd_chunk),
        out_specs=(pl.BlockSpec((CHUNK, gw), lambda b, g, c: (b * nc + c, g)),
                   pl.BlockSpec((CHUNK, gw), lambda b, g, c: (b * nc + nc - 1 - c, g))),
        scratch_shapes=[pltpu.VMEM((2, GROUP_HEADS // 2, SSM_STATE, LANE), F32)],
        compiler_params=_params(3),
        name="ssd_scan",
    )(*args, *args)


def _ssd_finish_kernel(yf_ref, yb_ref, xs_ref, z_ref, d_ref, g_ref, o_ref, *, row_chunk):
    rows, width = o_ref.shape

    def body(r, carry):
        sl = pl.ds(pl.multiple_of(r * row_chunk, row_chunk), row_chunk)
        z = z_ref[sl, :]
        y = (yf_ref[sl, :] + yb_ref[sl, :]) + xs_ref[sl, :] * d_ref[...]
        y = y * (z * jax.nn.sigmoid(z))
        ms = jnp.sum(y * y, axis=-1, keepdims=True) / width
        o_ref[sl, :] = (y * lax.rsqrt(ms + NORM_EPS) * g_ref[...]).astype(BF16)
        return carry

    lax.fori_loop(0, rows // row_chunk, body, 0)


def _ssd_finish(yf, yb, xbc, proj, d_exp, gain, cfg):
    rows = yf.shape[0]
    tm = CHUNK
    w = cfg.ssm_w
    row_spec = pl.BlockSpec((tm, w), lambda i: (i, 0))
    vec_spec = pl.BlockSpec((1, w), lambda i: (0, 0))
    return pl.pallas_call(
        functools.partial(_ssd_finish_kernel, row_chunk=32),
        out_shape=jax.ShapeDtypeStruct((rows, w), BF16),
        grid=(rows // tm,),
        in_specs=[row_spec, row_spec, row_spec, row_spec, vec_spec, vec_spec],
        out_specs=row_spec,
        compiler_params=_params(1),
        name="ssd_finish",
    )(yf, yb, xbc, proj, d_exp, gain)


def _out_proj_kernel(oa_ref, os_ref, wa_ref, ws_ref, h_ref, o_ref):
    acc = jnp.dot(oa_ref[...], wa_ref[...], preferred_element_type=F32)
    acc = acc + jnp.dot(os_ref[...], ws_ref[...], preferred_element_type=F32)
    o_ref[...] = h_ref[...] + acc


def _out_proj(o_att, o_ssm, w, h, cfg):
    rows, d = h.shape
    tm = _pick(rows, 512, LANE)
    tn = _pick(d, 512, LANE)
    kw = cfg.att_w
    return pl.pallas_call(
        _out_proj_kernel,
        out_shape=jax.ShapeDtypeStruct((rows, d), F32),
        grid=(rows // tm, d // tn),
        in_specs=[pl.BlockSpec((tm, kw), lambda i, j: (i, 0)),
                  pl.BlockSpec((tm, kw), lambda i, j: (i, 0)),
                  pl.BlockSpec((kw, tn), lambda i, j: (0, j)),
                  pl.BlockSpec((kw, tn), lambda i, j: (1, j)),
                  pl.BlockSpec((tm, tn), lambda i, j: (i, j))],
        out_specs=pl.BlockSpec((tm, tn), lambda i, j: (i, j)),
        input_output_aliases={4: 0},
        compiler_params=_params(2),
        name="out_proj",
    )(o_att, o_ssm, w, w, h)


def _final_norm_kernel(x_ref, g_ref, o_ref, *, row_chunk):
    rows, width = o_ref.shape

    def body(r, carry):
        sl = pl.ds(pl.multiple_of(r * row_chunk, row_chunk), row_chunk)
        x = x_ref[sl, :]
        ms = jnp.sum(x * x, axis=-1, keepdims=True) / width
        o_ref[sl, :] = x * lax.rsqrt(ms + NORM_EPS) * g_ref[...]
        return carry

    lax.fori_loop(0, rows // row_chunk, body, 0)


def _final_norm(h, gain, batch, lp):
    d = h.shape[1]
    nc = lp // CHUNK
    seq = lp - CHUNK
    return pl.pallas_call(
        functools.partial(_final_norm_kernel, row_chunk=32),
        out_shape=jax.ShapeDtypeStruct((batch * seq, d), F32),
        grid=(batch, nc - 1),
        in_specs=[pl.BlockSpec((CHUNK, d), lambda b, c: (b * nc + 1 + c, 0)),
                  pl.BlockSpec((1, d), lambda b, c: (0, 0))],
        out_specs=pl.BlockSpec((CHUNK, d), lambda b, c: (b * (nc - 1) + c, 0)),
        compiler_params=_params(2),
        name="final_norm",
    )(h, gain)


def _prep_layer_weights(cfg, w_in, w_q_up, w_kv_up, w_out, dt_bias, a_log, d_skip):
    q, kv, hs = cfg.q_lora, cfg.kv_lora, cfg.ssm_heads
    d = cfg.d_model
    o = 0
    w_q = w_in[:, o:o + q]; o += q
    w_kv = w_in[:, o:o + kv]; o += kv
    w_kr = w_in[:, o:o + QK_ROPE]; o += QK_ROPE
    w_gate = w_in[:, o:o + cfg.att_w]; o += cfg.att_w
    w_z = w_in[:, o:o + cfg.ssm_w]; o += cfg.ssm_w
    w_xbc = w_in[:, o:o + cfg.xbc_w]; o += cfg.xbc_w
    w_dt = w_in[:, o:o + 2 * hs]

    def pad_dirs(t):
        z = jnp.zeros(t.shape[:-1] + (DT_BWD - hs,), t.dtype)
        return jnp.concatenate([t[..., :hs], z, t[..., hs:], z], axis=-1)

    w_in_p = jnp.concatenate(
        [w_z, w_gate, w_xbc, w_kv, w_q, pad_dirs(w_dt), w_kr,
         jnp.zeros((d, LANE - QK_ROPE), w_in.dtype)], axis=1).astype(BF16)

    wq = w_q_up.reshape(q, cfg.att_heads // 2, 2, QK_NOPE + QK_ROPE)
    wq_p = jnp.concatenate(
        [wq[:, :, 0, :QK_NOPE], wq[:, :, 1, :QK_NOPE], wq[:, :, 0, QK_NOPE:], wq[:, :, 1, QK_NOPE:]],
        axis=-1).reshape(q, cfg.q_w).astype(BF16)

    wkv = w_kv_up.reshape(kv, cfg.att_heads, QK_NOPE + V_DIM)
    wk_t = wkv[:, :, :QK_NOPE].reshape(kv, cfg.att_heads * QK_NOPE).T.astype(BF16)
    wv = wkv[:, :, QK_NOPE:].reshape(kv, cfg.att_w).astype(BF16)

    return dict(
        w_in=w_in_p, w_q=wq_p, wk_t=wk_t, wv=wv, w_out=w_out.astype(BF16),
        dt_bias=pad_dirs(dt_bias.reshape(1, 2 * hs)),
        a_log=pad_dirs(a_log.reshape(1, 2 * hs)),
        d_exp=jnp.repeat(d_skip, SSM_HEADDIM).reshape(1, cfg.ssm_w))


def _run_trunk(x, meta_tokens, layers, final_norm, nq, cfg):
    batch, seq, d = x.shape
    assert seq % ATT_KEY_CHUNK == 0 or seq < ATT_KEY_CHUNK
    lp = seq + CHUNK
    h = jnp.concatenate(
        [jnp.zeros((batch, PAD, d), x.dtype),
         jnp.broadcast_to(meta_tokens[None].astype(x.dtype), (batch, N_META, d)), x],
        axis=1).reshape(batch * lp, d)
    cos_t, sin_t = _rope_tables(batch, lp)
    for lw in layers:
        proj = _in_proj(h, lw["norm_mix"], lw["w_in"], cfg)
        q = _q_up(proj, lw["q_norm"], lw["w_q"], cos_t, sin_t, cfg)
        kr_t = _k_rope(proj, cos_t, sin_t, cfg)
        kt, v = _kv_up(proj, lw["kv_norm"], lw["wk_t"], lw["wv"], kr_t, cfg)
        o_att = _attention(q, kt, v, proj, batch, lp, nq, cfg)
        xbc = _conv(proj, lw["conv_w"], lw["conv_b"], cfg)
        cum, cum_t, dt_t, w_t, etot = _ssd_prep(proj, lw["dt_bias"], lw["a_log"], batch, lp, cfg)
        yf, yb = _ssd(xbc, cum, cum_t, dt_t, w_t, etot, batch, lp, cfg)
        o_ssm = _ssd_finish(yf, yb, xbc, proj, lw["d_exp"], lw["ssm_norm"], cfg)
        h = _out_proj(o_att, o_ssm, lw["w_out"], h, cfg)
    y = _final_norm(h, final_norm.reshape(1, d), batch, lp)
    return y.reshape(batch, seq, d)


def _query_blocks(lp):
    for nq in range(1, lp // 16 + 1):
        if lp % nq == 0 and (lp // nq) % 16 == 0 and lp // nq <= 576:
            return nq
    raise ValueError(lp)


def _forward(cfg, x_prompt, x_sample, meta_tokens, norm_mix, w_in, q_norm, w_q_up, kv_norm,
             w_kv_up, conv_w, conv_b, dt_bias, a_log, d_skip, ssm_norm, w_out, final_norm):
    _check_cfg(cfg)
    depth = w_in.shape[0]
    layers = []
    for i in range(depth):
        lw = _prep_layer_weights(cfg, w_in[i], w_q_up[i], w_kv_up[i], w_out[i],
                                 dt_bias[i], a_log[i], d_skip[i])
        lw.update(norm_mix=norm_mix[i].reshape(1, -1), q_norm=q_norm[i].reshape(1, -1),
                  kv_norm=kv_norm[i].reshape(1, -1), conv_w=conv_w[i],
                  conv_b=conv_b[i].reshape(1, -1), ssm_norm=ssm_norm[i].reshape(1, -1))
        layers.append(lw)
    outs = []
    for x in (x_prompt, x_sample):
        nq = _query_blocks(x.shape[1] + CHUNK)
        outs.append(_run_trunk(x, meta_tokens, layers, final_norm, nq, cfg))
    return tuple(outs)


def kernel(x_prompt, x_sample, meta_tokens, norm_mix, w_in, q_norm, w_q_up, kv_norm, w_kv_up,
           conv_w, conv_b, dt_bias, a_log, d_skip, ssm_norm, w_out, final_norm):
    return _forward(Cfg(), x_prompt, x_sample, meta_tokens, norm_mix, w_in, q_norm, w_q_up,
                    kv_norm, w_kv_up, conv_w, conv_b, dt_bias, a_log, d_skip, ssm_norm, w_out,
                    final_norm)
```

```python
import functools
import math
from typing import NamedTuple

import jax
import jax.numpy as jnp
from jax import lax
from jax.experimental import pallas as pl
from jax.experimental.pallas import tpu as pltpu

F32 = jnp.float32
BF16 = jnp.bfloat16

N_META = 16
NORM_EPS = 1e-6
QK_NOPE = 128
QK_ROPE = 64
V_DIM = 128
ROPE_THETA = 10000.0
SSM_HEADDIM = 64
SSM_STATE = 128
CONV_WIDTH = 5
CONV_HALF = CONV_WIDTH // 2
CHUNK = 128
PAD = CHUNK - N_META
GROUP_HEADS = 8
DT_LANES = 128
DT_BWD = 64
LANE = 128
SUBLANE = 8
MXU_COLS = 256
Q_PRESCALE = math.log2(math.e) / math.sqrt(QK_NOPE + QK_ROPE)
VMEM_LIMIT = 56 * 1024 * 1024
ATT_KEY_CHUNK = 1024
SSD_GROUPS_PER_STEP = 4


class Cfg(NamedTuple):
    d_model: int = 4096
    att_heads: int = 32
    q_lora: int = 1536
    kv_lora: int = 512
    ssm_heads: int = 64
    ssm_groups: int = 8

    @property
    def att_w(self):
        return self.att_heads * V_DIM

    @property
    def ssm_w(self):
        return self.ssm_heads * SSM_HEADDIM

    @property
    def gn(self):
        return self.ssm_groups * SSM_STATE

    @property
    def xbc_w(self):
        return self.ssm_w + 2 * self.gn

    @property
    def q_piece(self):
        return self.q_lora // 3

    @property
    def off_z(self):
        return 0

    @property
    def off_gate(self):
        return self.ssm_w

    @property
    def off_xbc(self):
        return self.off_gate + self.att_w

    @property
    def off_kv(self):
        return self.off_xbc + self.xbc_w

    @property
    def off_q(self):
        return self.off_kv + self.kv_lora

    @property
    def off_dt(self):
        return self.off_q + self.q_lora

    @property
    def off_kr(self):
        return self.off_dt + DT_LANES

    @property
    def proj_w(self):
        return self.off_kr + LANE

    @property
    def q_w(self):
        return (self.att_heads // 2) * 384


def _check_cfg(cfg):
    assert cfg.ssm_w == cfg.d_model == cfg.att_w
    assert cfg.ssm_heads == cfg.ssm_groups * GROUP_HEADS and cfg.ssm_heads <= DT_BWD
    assert cfg.att_heads % 2 == 0 and cfg.q_lora % 3 == 0 and cfg.q_piece % LANE == 0
    assert cfg.kv_lora % LANE == 0
    assert cfg.off_kv % cfg.kv_lora == 0 and cfg.off_q % cfg.q_piece == 0


def _pick(n, target, unit):
    best = None
    for t in range(unit, min(n, target) + 1, unit):
        if n % t == 0:
            best = t
    assert best is not None, (n, target, unit)
    return best


def _params(n_axes):
    return pltpu.CompilerParams(dimension_semantics=("arbitrary",) * n_axes,
                                vmem_limit_bytes=VMEM_LIMIT)


def _rmsnorm_rows(dst_ref, piece_refs, gain_refs, rows, row_chunk=64):
    width = sum(p.shape[1] for p in piece_refs)
    row_chunk = min(row_chunk, rows)

    def body(r, carry):
        sl = pl.ds(pl.multiple_of(r * row_chunk, row_chunk), row_chunk)
        xs = [p[sl, :] for p in piece_refs]
        ss = xs[0] * xs[0]
        ss = jnp.sum(ss, axis=-1, keepdims=True)
        for x in xs[1:]:
            ss = ss + jnp.sum(x * x, axis=-1, keepdims=True)
        rstd = lax.rsqrt(ss / width + NORM_EPS)
        off = 0
        for x, g in zip(xs, gain_refs):
            w = x.shape[1]
            dst_ref[sl, off:off + w] = (x * rstd * g[...]).astype(BF16)
            off += w
        return carry

    lax.fori_loop(0, rows // row_chunk, body, 0)


def _in_proj_kernel(x_ref, g_ref, w_ref, o_ref, u_ref):
    @pl.when(pl.program_id(1) == 0)
    def _():
        _rmsnorm_rows(u_ref, [x_ref], [g_ref], x_ref.shape[0])

    o_ref[...] = jnp.dot(u_ref[...], w_ref[...], preferred_element_type=F32)


def _in_proj(h, gain, w, cfg):
    rows, d = h.shape
    n = w.shape[1]
    tm = _pick(rows, 512, LANE)
    tn = _pick(n, 1280, MXU_COLS)
    return pl.pallas_call(
        _in_proj_kernel,
        out_shape=jax.ShapeDtypeStruct((rows, n), F32),
        grid=(rows // tm, n // tn),
        in_specs=[pl.BlockSpec((tm, d), lambda i, j: (i, 0)),
                  pl.BlockSpec((1, d), lambda i, j: (0, 0)),
                  pl.BlockSpec((d, tn), lambda i, j: (0, j))],
        out_specs=pl.BlockSpec((tm, tn), lambda i, j: (i, j)),
        scratch_shapes=[pltpu.VMEM((tm, d), BF16)],
        compiler_params=_params(2),
        name="in_proj",
    )(h, gain, w)


def _rope128(r, cos_t, sin_t):
    lane = lax.broadcasted_iota(jnp.int32, (1, LANE), 1)
    first_half = (lane % QK_ROPE) < (QK_ROPE // 2)
    partner = jnp.where(first_half,
                        pltpu.roll(r, LANE - QK_ROPE // 2, 1),
                        pltpu.roll(r, QK_ROPE // 2, 1))
    return r * cos_t + partner * sin_t


def _rope_tables(batch, lp):
    half = QK_ROPE // 2
    inv = 1.0 / (ROPE_THETA ** (jnp.arange(half, dtype=F32) / half))
    pos = jnp.maximum(jnp.arange(lp, dtype=F32) - PAD, 0.0)
    ang = pos[:, None] * inv[None, :]
    cos, sin = jnp.cos(ang), jnp.sin(ang)
    cos_t = jnp.tile(cos, (batch, LANE // half))
    sin_t = jnp.tile(jnp.concatenate([-sin, sin], axis=1), (batch, LANE // QK_ROPE))
    return cos_t, sin_t


def _q_up_kernel(x0_ref, x1_ref, x2_ref, g0_ref, g1_ref, g2_ref, w_ref, cos_ref, sin_ref,
                 o_ref, u_ref):
    @pl.when(pl.program_id(1) == 0)
    def _():
        _rmsnorm_rows(u_ref, [x0_ref, x1_ref, x2_ref], [g0_ref, g1_ref, g2_ref],
                      x0_ref.shape[0])

    acc = jnp.dot(u_ref[...], w_ref[...], preferred_element_type=F32) * Q_PRESCALE
    pair_w = 2 * QK_NOPE + LANE
    for p in range(acc.shape[1] // pair_w):
        base = p * pair_w
        o_ref[:, base:base + 2 * QK_NOPE] = acc[:, base:base + 2 * QK_NOPE].astype(BF16)
        r = acc[:, base + 2 * QK_NOPE:base + pair_w]
        o_ref[:, base + 2 * QK_NOPE:base + pair_w] = _rope128(
            r, cos_ref[...], sin_ref[...]).astype(BF16)


def _q_up(proj, gain, w, cos_t, sin_t, cfg):
    rows = proj.shape[0]
    tm = _pick(rows, 512, LANE)
    qp = cfg.q_piece
    tn = _pick(cfg.q_w, 1536, 768)
    pb = cfg.off_q // qp
    x_specs = [pl.BlockSpec((tm, qp), functools.partial(lambda i, j, k: (i, pb + k), k=k))
               for k in range(3)]
    g_specs = [pl.BlockSpec((1, qp), functools.partial(lambda i, j, k: (0, k), k=k))
               for k in range(3)]
    return pl.pallas_call(
        _q_up_kernel,
        out_shape=jax.ShapeDtypeStruct((rows, cfg.q_w), BF16),
        grid=(rows // tm, cfg.q_w // tn),
        in_specs=x_specs + g_specs + [
            pl.BlockSpec((cfg.q_lora, tn), lambda i, j: (0, j)),
            pl.BlockSpec((tm, LANE), lambda i, j: (i, 0)),
            pl.BlockSpec((tm, LANE), lambda i, j: (i, 0))],
        out_specs=pl.BlockSpec((tm, tn), lambda i, j: (i, j)),
        scratch_shapes=[pltpu.VMEM((tm, cfg.q_lora), BF16)],
        compiler_params=_params(2),
        name="q_up",
    )(proj, proj, proj, gain, gain, gain, w, cos_t, sin_t)


def _k_rope_kernel(x_ref, cos_ref, sin_ref, o_ref):
    r = _rope128(x_ref[...], cos_ref[...], sin_ref[...])
    o_ref[...] = r.T[:QK_ROPE, :].astype(BF16)


def _k_rope(proj, cos_t, sin_t, cfg):
    rows = proj.shape[0]
    tm = _pick(rows, 512, LANE)
    cb = cfg.off_kr // LANE
    return pl.pallas_call(
        _k_rope_kernel,
        out_shape=jax.ShapeDtypeStruct((QK_ROPE, rows), BF16),
        grid=(rows // tm,),
        in_specs=[pl.BlockSpec((tm, LANE), lambda i: (i, cb)),
                  pl.BlockSpec((tm, LANE), lambda i: (i, 0)),
                  pl.BlockSpec((tm, LANE), lambda i: (i, 0))],
        out_specs=pl.BlockSpec((QK_ROPE, tm), lambda i: (0, i)),
        compiler_params=_params(1),
        name="k_rope",
    )(proj, cos_t, sin_t)


def _kv_up_kernel(x_ref, g_ref, wk_ref, wv_ref, kr_ref, kt_ref, v_ref, u_ref):
    @pl.when(pl.program_id(1) == 0)
    def _():
        _rmsnorm_rows(u_ref, [x_ref], [g_ref], x_ref.shape[0])

    u = u_ref[...]
    kn = lax.dot_general(wk_ref[...], u, (((1,), (1,)), ((), ())),
                         preferred_element_type=F32)
    kr = kr_ref[...]
    zero = jnp.zeros_like(kr)
    heads = wk_ref.shape[0] // QK_NOPE
    for t in range(heads):
        base = t * 2 * QK_NOPE
        kt_ref[base:base + QK_NOPE, :] = kn[t * QK_NOPE:(t + 1) * QK_NOPE, :].astype(BF16)
        slot_a, slot_b = (kr, zero) if t % 2 == 0 else (zero, kr)
        kt_ref[base + QK_NOPE:base + QK_NOPE + QK_ROPE, :] = slot_a
        kt_ref[base + QK_NOPE + QK_ROPE:base + 2 * QK_NOPE, :] = slot_b
    v_ref[...] = jnp.dot(u, wv_ref[...], preferred_element_type=F32).astype(BF16)


def _kv_up(proj, gain, wk_t, wv, kr_t, cfg):
    rows = proj.shape[0]
    tm = _pick(rows, 512, LANE)
    hb = 8 if cfg.att_heads % 8 == 0 else 2
    kvb = cfg.off_kv // cfg.kv_lora
    return pl.pallas_call(
        _kv_up_kernel,
        out_shape=(jax.ShapeDtypeStruct((cfg.att_heads * 2 * QK_NOPE, rows), BF16),
                   jax.ShapeDtypeStruct((rows, cfg.att_w), BF16)),
        grid=(rows // tm, cfg.att_heads // hb),
        in_specs=[pl.BlockSpec((tm, cfg.kv_lora), lambda i, j: (i, kvb)),
                  pl.BlockSpec((1, cfg.kv_lora), lambda i, j: (0, 0)),
                  pl.BlockSpec((hb * QK_NOPE, cfg.kv_lora), lambda i, j: (j, 0)),
                  pl.BlockSpec((cfg.kv_lora, hb * V_DIM), lambda i, j: (0, j)),
                  pl.BlockSpec((QK_ROPE, tm), lambda i, j: (0, i))],
        out_specs=(pl.BlockSpec((hb * 2 * QK_NOPE, tm), lambda i, j: (j, i)),
                   pl.BlockSpec((tm, hb * V_DIM), lambda i, j: (i, j))),
        scratch_shapes=[pltpu.VMEM((tm, cfg.kv_lora), BF16)],
        compiler_params=_params(2),
        name="kv_up",
    )(proj, gain, wk_t, wv, kr_t)


def _attention_kernel(qn_ref, qr_ref, kt_ref, v_ref, gate_ref, o_ref, *, key_chunk):
    q = jnp.concatenate([qn_ref[...], qr_ref[...]], axis=1)
    lp = kt_ref.shape[1]

    def weighted_values(s, m_new, a, size):
        p = jnp.exp2(s - m_new).astype(BF16)
        v_ones = jnp.concatenate([v_ref[a:a + size, :], jnp.ones((size, V_DIM), BF16)], axis=1)
        return jnp.dot(p, v_ones, preferred_element_type=F32)

    s = jnp.dot(q, kt_ref[:, 0:CHUNK], preferred_element_type=F32)
    col = lax.broadcasted_iota(jnp.int32, (1, CHUNK), 1)
    s = jnp.where(col >= PAD, s, -jnp.inf)
    m = jnp.max(s, axis=1, keepdims=True)
    acc = weighted_values(s, m, 0, CHUNK)

    for a in range(CHUNK, lp, key_chunk):
        s = jnp.dot(q, kt_ref[:, a:a + key_chunk], preferred_element_type=F32)
        m_new = jnp.maximum(m, jnp.max(s, axis=1, keepdims=True))
        acc = jnp.exp2(m - m_new) * acc + weighted_values(s, m_new, a, key_chunk)
        m = m_new

    g = gate_ref[...]
    o_ref[...] = ((acc[:, :V_DIM] / acc[:, V_DIM:]) * (g * jax.nn.sigmoid(g))).astype(BF16)


def _attention(q, kt, v, proj, batch, lp, nq, cfg):
    rows = q.shape[0]
    tq = lp // nq
    assert lp % nq == 0 and tq % 16 == 0
    key_chunk = _pick(lp - CHUNK, ATT_KEY_CHUNK, LANE)
    gate_b = cfg.off_gate // LANE
    kern = functools.partial(_attention_kernel, key_chunk=key_chunk)
    return pl.pallas_call(
        kern,
        out_shape=jax.ShapeDtypeStruct((rows, cfg.att_w), BF16),
        grid=(batch, cfg.att_heads, nq),
        in_specs=[
            pl.BlockSpec((tq, QK_NOPE), lambda b, h, i: (b * nq + i, (h // 2) * 3 + h % 2)),
            pl.BlockSpec((tq, LANE), lambda b, h, i: (b * nq + i, (h // 2) * 3 + 2)),
            pl.BlockSpec((2 * QK_NOPE, lp), lambda b, h, i: (h, b)),
            pl.BlockSpec((lp, V_DIM), lambda b, h, i: (b, h)),
            pl.BlockSpec((tq, V_DIM), lambda b, h, i: (b * nq + i, gate_b + h))],
        out_specs=pl.BlockSpec((tq, V_DIM), lambda b, h, i: (b * nq + i, h)),
        compiler_params=_params(3),
        name="attention",
    )(q, q, kt, v, proj)


def _conv_kernel(x_ref, prev_ref, next_ref, w_ref, b_ref, o_ref, ext_ref):
    i = pl.program_id(0)
    tm = x_ref.shape[0]
    ext_ref[0:SUBLANE, :] = jnp.where(i == 0, 0.0, prev_ref[...])
    ext_ref[SUBLANE:SUBLANE + tm, :] = x_ref[...]
    ext_ref[SUBLANE + tm:2 * SUBLANE + tm, :] = jnp.where(
        i == pl.num_programs(0) - 1, 0.0, next_ref[...])
    acc = b_ref[...]
    for tap in range(CONV_WIDTH):
        acc = acc + ext_ref[pl.ds(SUBLANE - CONV_HALF + tap, tm), :] * w_ref[tap:tap + 1, :]
    o_ref[...] = acc * jax.nn.sigmoid(acc)


def _conv(proj, w, b, cfg):
    rows = proj.shape[0]
    tm = _pick(rows, 512, LANE)
    tc = _pick(cfg.xbc_w, 512, LANE)
    assert cfg.off_xbc % tc == 0
    cb = cfg.off_xbc // tc
    hb = tm // SUBLANE
    last = rows // SUBLANE - 1
    return pl.pallas_call(
        _conv_kernel,
        out_shape=jax.ShapeDtypeStruct((rows, cfg.xbc_w), F32),
        grid=(rows // tm, cfg.xbc_w // tc),
        in_specs=[pl.BlockSpec((tm, tc), lambda i, j: (i, cb + j)),
                  pl.BlockSpec((SUBLANE, tc), lambda i, j: (jnp.maximum(i * hb - 1, 0), cb + j)),
                  pl.BlockSpec((SUBLANE, tc),
                               lambda i, j: (jnp.minimum((i + 1) * hb, last), cb + j)),
                  pl.BlockSpec((CONV_WIDTH, tc), lambda i, j: (0, j)),
                  pl.BlockSpec((1, tc), lambda i, j: (0, j))],
        out_specs=pl.BlockSpec((tm, tc), lambda i, j: (i, j)),
        scratch_shapes=[pltpu.VMEM((tm + 2 * SUBLANE, tc), F32)],
        compiler_params=_params(2),
        name="conv_silu",
    )(proj, proj, proj, w, b)


def _ssd_prep_kernel(raw_ref, bias_ref, alog_ref, cum_ref, cumt_ref, dtt_ref, wt_ref, etot_ref):
    x = raw_ref[...] + bias_ref[...]
    dt = jnp.maximum(x, 0.0) + jnp.log1p(jnp.exp(-jnp.abs(x)))
    row = lax.broadcasted_iota(jnp.int32, (CHUNK, DT_LANES), 0)
    lane = lax.broadcasted_iota(jnp.int32, (CHUNK, DT_LANES), 1)
    dt = jnp.where(jnp.logical_and(pl.program_id(1) == 0, row < PAD), 0.0, dt)
    da = dt * (-jnp.exp(alog_ref[...]))
    fwd = da
    bwd = da
    k = 1
    while k < CHUNK:
        fwd = fwd + jnp.where(row >= k, pltpu.roll(fwd, k, 0), 0.0)
        bwd = bwd + jnp.where(row < CHUNK - k, pltpu.roll(bwd, CHUNK - k, 0), 0.0)
        k *= 2
    is_fwd = lane < DT_BWD
    cum = jnp.where(is_fwd, fwd, bwd)
    tot = jnp.where(is_fwd[0:1, :], fwd[CHUNK - 1:CHUNK, :], bwd[0:1, :])
    w = jnp.exp(tot - cum) * dt
    cum_ref[...] = cum
    cumt_ref[...] = cum.T
    dtt_ref[...] = dt.T
    wt_ref[...] = w.T
    etot_ref[...] = jnp.broadcast_to(jnp.exp(tot), (CHUNK, DT_LANES)).T


def _ssd_prep(proj, bias, alog, batch, lp, cfg):
    rows = proj.shape[0]
    nc = lp // CHUNK
    cb = cfg.off_dt // DT_LANES
    row_major = pl.BlockSpec((CHUNK, DT_LANES), lambda b, c: (b * nc + c, 0))
    col_major = pl.BlockSpec((DT_LANES, CHUNK), lambda b, c: (0, b * nc + c))
    return pl.pallas_call(
        _ssd_prep_kernel,
        out_shape=(jax.ShapeDtypeStruct((rows, DT_LANES), F32),
                   jax.ShapeDtypeStruct((DT_LANES, rows), F32),
                   jax.ShapeDtypeStruct((DT_LANES, rows), F32),
                   jax.ShapeDtypeStruct((DT_LANES, rows), F32),
                   jax.ShapeDtypeStruct((rows, DT_LANES), F32)),
        grid=(batch, nc),
        in_specs=[pl.BlockSpec((CHUNK, DT_LANES), lambda b, c: (b * nc + c, cb)),
                  pl.BlockSpec((1, DT_LANES), lambda b, c: (0, 0)),
                  pl.BlockSpec((1, DT_LANES), lambda b, c: (0, 0))],
        out_specs=(row_major, col_major, col_major, col_major, row_major),
        compiler_params=_params(2),
        name="ssd_prep",
    )(proj, bias, alog)


def _ssd_direction(reverse, group, head_base, xs_ref, bm_ref, cm_ref, cum_ref, cumt_ref,
                   dtt_ref, wt_ref, etot_ref, d_ref, y_ref, state_ref):
    gcols = slice(group * SSM_STATE, (group + 1) * SSM_STATE)
    bm = bm_ref[:, gcols]
    cm = cm_ref[:, gcols]
    bm_t = bm.T
    cb = jnp.dot(cm.astype(BF16), bm_t.astype(BF16), preferred_element_type=F32)
    row = lax.broadcasted_iota(jnp.int32, (CHUNK, CHUNK), 0)
    col = lax.broadcasted_iota(jnp.int32, (CHUNK, CHUNK), 1)
    live = (col >= row) if reverse else (row >= col)
    lane = lax.broadcasted_iota(jnp.int32, (CHUNK, DT_LANES), 1)
    low_half = lax.broadcasted_iota(jnp.int32, (1, LANE), 1) < SSM_HEADDIM
    cum = cum_ref[...]
    for pair in range(GROUP_HEADS // 2):
        cols = slice((group * GROUP_HEADS // 2 + pair) * LANE,
                     (group * GROUP_HEADS // 2 + pair + 1) * LANE)
        xs_f32 = xs_ref[:, cols]
        xs_pair = xs_f32.astype(BF16)
        state = state_ref[group * GROUP_HEADS // 2 + pair]
        rhs = jnp.concatenate([xs_pair, state.astype(BF16)], axis=0)
        ys, sts, ets = [], [], []
        for t in range(2):
            hd = head_base + 2 * pair + t
            a_col = jnp.sum(jnp.where(lane == hd, cum, 0.0), axis=1, keepdims=True)
            a_row = cumt_ref[pl.ds(hd, 1), :]
            decay = jnp.exp(jnp.where(live, a_col - a_row, -jnp.inf))
            mix = cb * decay * dtt_ref[pl.ds(hd, 1), :]
            carry_in = cm * jnp.exp(a_col)
            lhs = jnp.concatenate([mix, carry_in], axis=1).astype(BF16)
            ys.append(jnp.dot(lhs, rhs, preferred_element_type=F32))
            weighted_b = (bm_t * wt_ref[pl.ds(hd, 1), :]).astype(BF16)
            sts.append(jnp.dot(weighted_b, xs_pair, preferred_element_type=F32))
            ets.append(etot_ref[pl.ds(hd, 1), :])
        y = jnp.where(low_half, ys[0], ys[1])
        if d_ref is not None:
            y = y + xs_f32 * d_ref[:, cols]
        y_ref[:, cols] = y
        state_ref[group * GROUP_HEADS // 2 + pair] = (
            state * jnp.where(low_half, ets[0], ets[1]) + jnp.where(low_half, sts[0], sts[1]))


def _ssd_kernel(*refs, groups_per_step):
    fwd_in, bwd_in = refs[0:8], refs[8:16]
    d_ref, yf_ref, yb_ref, state_ref = refs[16:20]

    @pl.when(pl.program_id(2) == 0)
    def _():
        state_ref[...] = jnp.zeros_like(state_ref)

    for group in range(groups_per_step):
        head_base = (pl.program_id(1) * groups_per_step + group) * GROUP_HEADS
        _ssd_direction(False, group, head_base, *fwd_in, d_ref, yf_ref, state_ref.at[0])
        _ssd_direction(True, group, DT_BWD + head_base, *bwd_in, None, yb_ref, state_ref.at[1])


def _ssd(xbc, cum, cum_t, dt_t, w_t, etot, d_exp, batch, lp, cfg):
    rows = xbc.shape[0]
    nc = lp // CHUNK
    gps = min(SSD_GROUPS_PER_STEP, cfg.ssm_groups)
    assert cfg.ssm_groups % gps == 0 and (cfg.ssm_w // SSM_STATE) % gps == 0
    gw = gps * GROUP_HEADS * SSM_HEADDIM
    sw = gps * SSM_STATE
    bm_b = cfg.ssm_w // sw
    cm_b = bm_b + cfg.ssm_groups // gps

    def specs(chunk_of):
        r = lambda b, c: b * nc + chunk_of(c)
        return [pl.BlockSpec((CHUNK, gw), lambda b, g, c: (r(b, c), g)),
                pl.BlockSpec((CHUNK, sw), lambda b, g, c: (r(b, c), bm_b + g)),
                pl.BlockSpec((CHUNK, sw), lambda b, g, c: (r(b, c), cm_b + g)),
                pl.BlockSpec((CHUNK, DT_LANES), lambda b, g, c: (r(b, c), 0)),
                pl.BlockSpec((DT_LANES, CHUNK), lambda b, g, c: (0, r(b, c))),
                pl.BlockSpec((DT_LANES, CHUNK), lambda b, g, c: (0, r(b, c))),
                pl.BlockSpec((DT_LANES, CHUNK), lambda b, g, c: (0, r(b, c))),
                pl.BlockSpec((CHUNK, DT_LANES), lambda b, g, c: (r(b, c), 0))]

    fwd_chunk = lambda c: c
    bwd_chunk = lambda c: nc - 1 - c
    args = (xbc, xbc, xbc, cum, cum_t, dt_t, w_t, etot)
    return pl.pallas_call(
        functools.partial(_ssd_kernel, groups_per_step=gps),
        out_shape=(jax.ShapeDtypeStruct((rows, cfg.ssm_w), F32),
                   jax.ShapeDtypeStruct((rows, cfg.ssm_w), F32)),
        grid=(batch, cfg.ssm_groups // gps, nc),
        in_specs=specs(fwd_chunk) + specs(bwd_chunk) + [
            pl.BlockSpec((1, gw), lambda b, g, c: (0, g))],
        out_specs=(pl.BlockSpec((CHUNK, gw), lambda b, g, c: (b * nc + c, g)),
                   pl.BlockSpec((CHUNK, gw), lambda b, g, c: (b * nc + nc - 1 - c, g))),
        scratch_shapes=[pltpu.VMEM((2, gps * GROUP_HEADS // 2, SSM_STATE, LANE), F32)],
        compiler_params=_params(3),
        name="ssd_scan",
    )(*args, *args, d_exp)


def _ssd_finish_kernel(yf_ref, yb_ref, z_ref, g_ref, o_ref, *, row_chunk):
    rows, width = o_ref.shape

    def body(r, carry):
        sl = pl.ds(pl.multiple_of(r * row_chunk, row_chunk), row_chunk)
        z = z_ref[sl, :]
        y = (yf_ref[sl, :] + yb_ref[sl, :]) * (z * jax.nn.sigmoid(z))
        ms = jnp.sum(y * y, axis=-1, keepdims=True) / width
        o_ref[sl, :] = (y * lax.rsqrt(ms + NORM_EPS) * g_ref[...]).astype(BF16)
        return carry

    lax.fori_loop(0, rows // row_chunk, body, 0)


def _ssd_finish(yf, yb, proj, gain, cfg):
    rows = yf.shape[0]
    tm = _pick(rows, 256, LANE)
    w = cfg.ssm_w
    row_spec = pl.BlockSpec((tm, w), lambda i: (i, 0))
    vec_spec = pl.BlockSpec((1, w), lambda i: (0, 0))
    return pl.pallas_call(
        functools.partial(_ssd_finish_kernel, row_chunk=32),
        out_shape=jax.ShapeDtypeStruct((rows, w), BF16),
        grid=(rows // tm,),
        in_specs=[row_spec, row_spec, row_spec, vec_spec],
        out_specs=row_spec,
        compiler_params=_params(1),
        name="ssd_finish",
    )(yf, yb, proj, gain)


def _out_proj_kernel(oa_ref, os_ref, wa_ref, ws_ref, h_ref, o_ref):
    acc = jnp.dot(oa_ref[...], wa_ref[...], preferred_element_type=F32)
    acc = acc + jnp.dot(os_ref[...], ws_ref[...], preferred_element_type=F32)
    o_ref[...] = h_ref[...] + acc


def _out_proj(o_att, o_ssm, w, h, cfg):
    rows, d = h.shape
    tm = _pick(rows, 512, LANE)
    tn = _pick(d, 512, LANE)
    kw = cfg.att_w
    return pl.pallas_call(
        _out_proj_kernel,
        out_shape=jax.ShapeDtypeStruct((rows, d), F32),
        grid=(rows // tm, d // tn),
        in_specs=[pl.BlockSpec((tm, kw), lambda i, j: (i, 0)),
                  pl.BlockSpec((tm, kw), lambda i, j: (i, 0)),
                  pl.BlockSpec((kw, tn), lambda i, j: (0, j)),
                  pl.BlockSpec((kw, tn), lambda i, j: (1, j)),
                  pl.BlockSpec((tm, tn), lambda i, j: (i, j))],
        out_specs=pl.BlockSpec((tm, tn), lambda i, j: (i, j)),
        input_output_aliases={4: 0},
        compiler_params=_params(2),
        name="out_proj",
    )(o_att, o_ssm, w, w, h)


def _final_norm_kernel(x_ref, g_ref, o_ref, *, row_chunk):
    rows, width = o_ref.shape

    def body(r, carry):
        sl = pl.ds(pl.multiple_of(r * row_chunk, row_chunk), row_chunk)
        x = x_ref[sl, :]
        ms = jnp.sum(x * x, axis=-1, keepdims=True) / width
        o_ref[sl, :] = x * lax.rsqrt(ms + NORM_EPS) * g_ref[...]
        return carry

    lax.fori_loop(0, rows // row_chunk, body, 0)


def _final_norm(h, gain, batch, lp):
    d = h.shape[1]
    nc = lp // CHUNK
    seq = lp - CHUNK
    return pl.pallas_call(
        functools.partial(_final_norm_kernel, row_chunk=32),
        out_shape=jax.ShapeDtypeStruct((batch * seq, d), F32),
        grid=(batch, nc - 1),
        in_specs=[pl.BlockSpec((CHUNK, d), lambda b, c: (b * nc + 1 + c, 0)),
                  pl.BlockSpec((1, d), lambda b, c: (0, 0))],
        out_specs=pl.BlockSpec((CHUNK, d), lambda b, c: (b * (nc - 1) + c, 0)),
        compiler_params=_params(2),
        name="final_norm",
    )(h, gain)


def _prep_layer_weights(cfg, w_in, w_q_up, w_kv_up, w_out, dt_bias, a_log, d_skip):
    q, kv, hs = cfg.q_lora, cfg.kv_lora, cfg.ssm_heads
    d = cfg.d_model
    o = 0
    w_q = w_in[:, o:o + q]; o += q
    w_kv = w_in[:, o:o + kv]; o += kv
    w_kr = w_in[:, o:o + QK_ROPE]; o += QK_ROPE
    w_gate = w_in[:, o:o + cfg.att_w]; o += cfg.att_w
    w_z = w_in[:, o:o + cfg.ssm_w]; o += cfg.ssm_w
    w_xbc = w_in[:, o:o + cfg.xbc_w]; o += cfg.xbc_w
    w_dt = w_in[:, o:o + 2 * hs]

    def pad_dirs(t):
        z = jnp.zeros(t.shape[:-1] + (DT_BWD - hs,), t.dtype)
        return jnp.concatenate([t[..., :hs], z, t[..., hs:], z], axis=-1)

    w_in_p = jnp.concatenate(
        [w_z, w_gate, w_xbc, w_kv, w_q, pad_dirs(w_dt), w_kr,
         jnp.zeros((d, LANE - QK_ROPE), w_in.dtype)], axis=1).astype(BF16)

    wq = w_q_up.reshape(q, cfg.att_heads // 2, 2, QK_NOPE + QK_ROPE)
    wq_p = jnp.concatenate(
        [wq[:, :, 0, :QK_NOPE], wq[:, :, 1, :QK_NOPE], wq[:, :, 0, QK_NOPE:], wq[:, :, 1, QK_NOPE:]],
        axis=-1).reshape(q, cfg.q_w).astype(BF16)

    wkv = w_kv_up.reshape(kv, cfg.att_heads, QK_NOPE + V_DIM)
    wk_t = wkv[:, :, :QK_NOPE].reshape(kv, cfg.att_heads * QK_NOPE).T.astype(BF16)
    wv = wkv[:, :, QK_NOPE:].reshape(kv, cfg.att_w).astype(BF16)

    return dict(
        w_in=w_in_p, w_q=wq_p, wk_t=wk_t, wv=wv, w_out=w_out.astype(BF16),
        dt_bias=pad_dirs(dt_bias.reshape(1, 2 * hs)),
        a_log=pad_dirs(a_log.reshape(1, 2 * hs)),
        d_exp=jnp.repeat(d_skip, SSM_HEADDIM).reshape(1, cfg.ssm_w))


def _run_trunk(x, meta_tokens, layers, final_norm, nq, cfg):
    batch, seq, d = x.shape
    assert seq % ATT_KEY_CHUNK == 0 or seq < ATT_KEY_CHUNK
    lp = seq + CHUNK
    h = jnp.concatenate(
        [jnp.zeros((batch, PAD, d), x.dtype),
         jnp.broadcast_to(meta_tokens[None].astype(x.dtype), (batch, N_META, d)), x],
        axis=1).reshape(batch * lp, d)
    cos_t, sin_t = _rope_tables(batch, lp)
    for lw in layers:
        proj = _in_proj(h, lw["norm_mix"], lw["w_in"], cfg)
        q = _q_up(proj, lw["q_norm"], lw["w_q"], cos_t, sin_t, cfg)
        kr_t = _k_rope(proj, cos_t, sin_t, cfg)
        kt, v = _kv_up(proj, lw["kv_norm"], lw["wk_t"], lw["wv"], kr_t, cfg)
        o_att = _attention(q, kt, v, proj, batch, lp, nq, cfg)
        xbc = _conv(proj, lw["conv_w"], lw["conv_b"], cfg)
        cum, cum_t, dt_t, w_t, etot = _ssd_prep(proj, lw["dt_bias"], lw["a_log"], batch, lp, cfg)
        yf, yb = _ssd(xbc, cum, cum_t, dt_t, w_t, etot, lw["d_exp"], batch, lp, cfg)
        o_ssm = _ssd_finish(yf, yb, proj, lw["ssm_norm"], cfg)
        h = _out_proj(o_att, o_ssm, lw["w_out"], h, cfg)
    y = _final_norm(h, final_norm.reshape(1, d), batch, lp)
    return y.reshape(batch, seq, d)


def _query_blocks(lp):
    for nq in range(1, lp // 16 + 1):
        if lp % nq == 0 and (lp // nq) % 16 == 0 and lp // nq <= 576:
            return nq
    raise ValueError(lp)


def _forward(cfg, x_prompt, x_sample, meta_tokens, norm_mix, w_in, q_norm, w_q_up, kv_norm,
             w_kv_up, conv_w, conv_b, dt_bias, a_log, d_skip, ssm_norm, w_out, final_norm):
    _check_cfg(cfg)
    depth = w_in.shape[0]
    layers = []
    for i in range(depth):
        lw = _prep_layer_weights(cfg, w_in[i], w_q_up[i], w_kv_up[i], w_out[i],
                                 dt_bias[i], a_log[i], d_skip[i])
        lw.update(norm_mix=norm_mix[i].reshape(1, -1), q_norm=q_norm[i].reshape(1, -1),
                  kv_norm=kv_norm[i].reshape(1, -1), conv_w=conv_w[i],
                  conv_b=conv_b[i].reshape(1, -1), ssm_norm=ssm_norm[i].reshape(1, -1))
        layers.append(lw)
    outs = []
    for x in (x_prompt, x_sample):
        nq = _query_blocks(x.shape[1] + CHUNK)
        outs.append(_run_trunk(x, meta_tokens, layers, final_norm, nq, cfg))
    return tuple(outs)


def kernel(x_prompt, x_sample, meta_tokens, norm_mix, w_in, q_norm, w_q_up, kv_norm, w_kv_up,
           conv_w, conv_b, dt_bias, a_log, d_skip, ssm_norm, w_out, final_norm):
    return _forward(Cfg(), x_prompt, x_sample, meta_tokens, norm_mix, w_in, q_norm, w_q_up,
                    kv_norm, w_kv_up, conv_w, conv_b, dt_bias, a_log, d_skip, ssm_norm, w_out,
                    final_norm)
```

```python
import functools
import math
from typing import NamedTuple

import jax
import jax.numpy as jnp
from jax import lax
from jax.experimental import pallas as pl
from jax.experimental.pallas import tpu as pltpu

F32 = jnp.float32
BF16 = jnp.bfloat16

N_META = 16
NORM_EPS = 1e-6
QK_NOPE = 128
QK_ROPE = 64
V_DIM = 128
ROPE_THETA = 10000.0
SSM_HEADDIM = 64
SSM_STATE = 128
CONV_WIDTH = 5
CONV_HALF = CONV_WIDTH // 2
CHUNK = 128
PAD = CHUNK - N_META
GROUP_HEADS = 8
DT_LANES = 128
DT_BWD = 64
LANE = 128
SUBLANE = 8
MXU_COLS = 256
LOG2_E = math.log2(math.e)
Q_PRESCALE = LOG2_E / math.sqrt(QK_NOPE + QK_ROPE)
VMEM_LIMIT = 56 * 1024 * 1024
ATT_KEY_CHUNK = 1024
SSD_GROUPS_PER_STEP = 4


class Cfg(NamedTuple):
    d_model: int = 4096
    att_heads: int = 32
    q_lora: int = 1536
    kv_lora: int = 512
    ssm_heads: int = 64
    ssm_groups: int = 8

    @property
    def att_w(self):
        return self.att_heads * V_DIM

    @property
    def ssm_w(self):
        return self.ssm_heads * SSM_HEADDIM

    @property
    def gn(self):
        return self.ssm_groups * SSM_STATE

    @property
    def xbc_w(self):
        return self.ssm_w + 2 * self.gn

    @property
    def q_piece(self):
        return self.q_lora // 3

    @property
    def off_z(self):
        return 0

    @property
    def off_gate(self):
        return self.ssm_w

    @property
    def off_xbc(self):
        return self.off_gate + self.att_w

    @property
    def off_kv(self):
        return self.off_xbc + self.xbc_w

    @property
    def off_q(self):
        return self.off_kv + self.kv_lora

    @property
    def off_dt(self):
        return self.off_q + self.q_lora

    @property
    def off_kr(self):
        return self.off_dt + DT_LANES

    @property
    def proj_w(self):
        return self.off_kr + LANE

    @property
    def q_w(self):
        return (self.att_heads // 2) * 384


def _check_cfg(cfg):
    assert cfg.ssm_w == cfg.d_model == cfg.att_w
    assert cfg.ssm_heads == cfg.ssm_groups * GROUP_HEADS and cfg.ssm_heads <= DT_BWD
    assert cfg.att_heads % 2 == 0 and cfg.q_lora % 3 == 0 and cfg.q_piece % LANE == 0
    assert cfg.kv_lora % LANE == 0
    assert cfg.off_kv % cfg.kv_lora == 0 and cfg.off_q % cfg.q_piece == 0


def _pick(n, target, unit):
    best = None
    for t in range(unit, min(n, target) + 1, unit):
        if n % t == 0:
            best = t
    assert best is not None, (n, target, unit)
    return best


def _params(n_axes):
    return pltpu.CompilerParams(dimension_semantics=("arbitrary",) * n_axes,
                                vmem_limit_bytes=VMEM_LIMIT)


def _rmsnorm_rows(dst_ref, piece_refs, gain_refs, rows, row_chunk=64):
    width = sum(p.shape[1] for p in piece_refs)
    row_chunk = min(row_chunk, rows)

    def body(r, carry):
        sl = pl.ds(pl.multiple_of(r * row_chunk, row_chunk), row_chunk)
        xs = [p[sl, :] for p in piece_refs]
        ss = xs[0] * xs[0]
        ss = jnp.sum(ss, axis=-1, keepdims=True)
        for x in xs[1:]:
            ss = ss + jnp.sum(x * x, axis=-1, keepdims=True)
        rstd = lax.rsqrt(ss / width + NORM_EPS)
        off = 0
        for x, g in zip(xs, gain_refs):
            w = x.shape[1]
            dst_ref[sl, off:off + w] = (x * rstd * g[...]).astype(BF16)
            off += w
        return carry

    lax.fori_loop(0, rows // row_chunk, body, 0)


def _in_proj_kernel(x_ref, g_ref, w_ref, o_ref, u_ref):
    @pl.when(pl.program_id(1) == 0)
    def _():
        _rmsnorm_rows(u_ref, [x_ref], [g_ref], x_ref.shape[0])

    o_ref[...] = jnp.dot(u_ref[...], w_ref[...], preferred_element_type=F32)


def _in_proj(h, gain, w, cfg):
    rows, d = h.shape
    n = w.shape[1]
    tm = _pick(rows, 512, LANE)
    tn = _pick(n, 1280, MXU_COLS)
    return pl.pallas_call(
        _in_proj_kernel,
        out_shape=jax.ShapeDtypeStruct((rows, n), F32),
        grid=(rows // tm, n // tn),
        in_specs=[pl.BlockSpec((tm, d), lambda i, j: (i, 0)),
                  pl.BlockSpec((1, d), lambda i, j: (0, 0)),
                  pl.BlockSpec((d, tn), lambda i, j: (0, j))],
        out_specs=pl.BlockSpec((tm, tn), lambda i, j: (i, j)),
        scratch_shapes=[pltpu.VMEM((tm, d), BF16)],
        compiler_params=_params(2),
        name="in_proj",
    )(h, gain, w)


def _rope128(r, cos_t, sin_t):
    lane = lax.broadcasted_iota(jnp.int32, (1, LANE), 1)
    first_half = (lane % QK_ROPE) < (QK_ROPE // 2)
    partner = jnp.where(first_half,
                        pltpu.roll(r, LANE - QK_ROPE // 2, 1),
                        pltpu.roll(r, QK_ROPE // 2, 1))
    return r * cos_t + partner * sin_t


def _rope_tables(batch, lp):
    half = QK_ROPE // 2
    inv = 1.0 / (ROPE_THETA ** (jnp.arange(half, dtype=F32) / half))
    pos = jnp.maximum(jnp.arange(lp, dtype=F32) - PAD, 0.0)
    ang = pos[:, None] * inv[None, :]
    cos, sin = jnp.cos(ang), jnp.sin(ang)
    cos_t = jnp.tile(cos, (batch, LANE // half))
    sin_t = jnp.tile(jnp.concatenate([-sin, sin], axis=1), (batch, LANE // QK_ROPE))
    return cos_t, sin_t


def _q_up_kernel(x0_ref, x1_ref, x2_ref, g0_ref, g1_ref, g2_ref, w_ref, cos_ref, sin_ref,
                 o_ref, u_ref):
    @pl.when(pl.program_id(1) == 0)
    def _():
        _rmsnorm_rows(u_ref, [x0_ref, x1_ref, x2_ref], [g0_ref, g1_ref, g2_ref],
                      x0_ref.shape[0])

    acc = jnp.dot(u_ref[...], w_ref[...], preferred_element_type=F32) * Q_PRESCALE
    pair_w = 2 * QK_NOPE + LANE
    for p in range(acc.shape[1] // pair_w):
        base = p * pair_w
        o_ref[:, base:base + 2 * QK_NOPE] = acc[:, base:base + 2 * QK_NOPE].astype(BF16)
        r = acc[:, base + 2 * QK_NOPE:base + pair_w]
        o_ref[:, base + 2 * QK_NOPE:base + pair_w] = _rope128(
            r, cos_ref[...], sin_ref[...]).astype(BF16)


def _q_up(proj, gain, w, cos_t, sin_t, cfg):
    rows = proj.shape[0]
    tm = _pick(rows, 512, LANE)
    qp = cfg.q_piece
    tn = _pick(cfg.q_w, 1536, 768)
    pb = cfg.off_q // qp
    x_specs = [pl.BlockSpec((tm, qp), functools.partial(lambda i, j, k: (i, pb + k), k=k))
               for k in range(3)]
    g_specs = [pl.BlockSpec((1, qp), functools.partial(lambda i, j, k: (0, k), k=k))
               for k in range(3)]
    return pl.pallas_call(
        _q_up_kernel,
        out_shape=jax.ShapeDtypeStruct((rows, cfg.q_w), BF16),
        grid=(rows // tm, cfg.q_w // tn),
        in_specs=x_specs + g_specs + [
            pl.BlockSpec((cfg.q_lora, tn), lambda i, j: (0, j)),
            pl.BlockSpec((tm, LANE), lambda i, j: (i, 0)),
            pl.BlockSpec((tm, LANE), lambda i, j: (i, 0))],
        out_specs=pl.BlockSpec((tm, tn), lambda i, j: (i, j)),
        scratch_shapes=[pltpu.VMEM((tm, cfg.q_lora), BF16)],
        compiler_params=_params(2),
        name="q_up",
    )(proj, proj, proj, gain, gain, gain, w, cos_t, sin_t)


def _k_rope_kernel(x_ref, cos_ref, sin_ref, o_ref):
    r = _rope128(x_ref[...], cos_ref[...], sin_ref[...])
    o_ref[...] = r.T[:QK_ROPE, :].astype(BF16)


def _k_rope(proj, cos_t, sin_t, cfg):
    rows = proj.shape[0]
    tm = _pick(rows, 512, LANE)
    cb = cfg.off_kr // LANE
    return pl.pallas_call(
        _k_rope_kernel,
        out_shape=jax.ShapeDtypeStruct((QK_ROPE, rows), BF16),
        grid=(rows // tm,),
        in_specs=[pl.BlockSpec((tm, LANE), lambda i: (i, cb)),
                  pl.BlockSpec((tm, LANE), lambda i: (i, 0)),
                  pl.BlockSpec((tm, LANE), lambda i: (i, 0))],
        out_specs=pl.BlockSpec((QK_ROPE, tm), lambda i: (0, i)),
        compiler_params=_params(1),
        name="k_rope",
    )(proj, cos_t, sin_t)


def _kv_up_kernel(x_ref, g_ref, wk_ref, wv_ref, kr_ref, kt_ref, v_ref, u_ref):
    @pl.when(pl.program_id(1) == 0)
    def _():
        _rmsnorm_rows(u_ref, [x_ref], [g_ref], x_ref.shape[0])

    u = u_ref[...]
    kn = lax.dot_general(wk_ref[...], u, (((1,), (1,)), ((), ())),
                         preferred_element_type=F32)
    kr = kr_ref[...]
    zero = jnp.zeros_like(kr)
    heads = wk_ref.shape[0] // QK_NOPE
    for t in range(heads):
        base = t * 2 * QK_NOPE
        kt_ref[base:base + QK_NOPE, :] = kn[t * QK_NOPE:(t + 1) * QK_NOPE, :].astype(BF16)
        slot_a, slot_b = (kr, zero) if t % 2 == 0 else (zero, kr)
        kt_ref[base + QK_NOPE:base + QK_NOPE + QK_ROPE, :] = slot_a
        kt_ref[base + QK_NOPE + QK_ROPE:base + 2 * QK_NOPE, :] = slot_b
    v_ref[...] = jnp.dot(u, wv_ref[...], preferred_element_type=F32).astype(BF16)


def _kv_up(proj, gain, wk_t, wv, kr_t, cfg):
    rows = proj.shape[0]
    tm = _pick(rows, 512, LANE)
    hb = 8 if cfg.att_heads % 8 == 0 else 2
    kvb = cfg.off_kv // cfg.kv_lora
    return pl.pallas_call(
        _kv_up_kernel,
        out_shape=(jax.ShapeDtypeStruct((cfg.att_heads * 2 * QK_NOPE, rows), BF16),
                   jax.ShapeDtypeStruct((rows, cfg.att_w), BF16)),
        grid=(rows // tm, cfg.att_heads // hb),
        in_specs=[pl.BlockSpec((tm, cfg.kv_lora), lambda i, j: (i, kvb)),
                  pl.BlockSpec((1, cfg.kv_lora), lambda i, j: (0, 0)),
                  pl.BlockSpec((hb * QK_NOPE, cfg.kv_lora), lambda i, j: (j, 0)),
                  pl.BlockSpec((cfg.kv_lora, hb * V_DIM), lambda i, j: (0, j)),
                  pl.BlockSpec((QK_ROPE, tm), lambda i, j: (0, i))],
        out_specs=(pl.BlockSpec((hb * 2 * QK_NOPE, tm), lambda i, j: (j, i)),
                   pl.BlockSpec((tm, hb * V_DIM), lambda i, j: (i, j))),
        scratch_shapes=[pltpu.VMEM((tm, cfg.kv_lora), BF16)],
        compiler_params=_params(2),
        name="kv_up",
    )(proj, gain, wk_t, wv, kr_t)


def _attention_kernel(q_ref, kt_ref, v_ref, gate_ref, o_ref, *, key_chunk):
    lp = kt_ref.shape[1]
    q_rope = q_ref[:, 2 * QK_NOPE:]
    for t in range(2):
        hcols = slice(t * V_DIM, (t + 1) * V_DIM)
        krows = slice(t * 2 * QK_NOPE, (t + 1) * 2 * QK_NOPE)
        q = jnp.concatenate([q_ref[:, hcols], q_rope], axis=1)

        def weighted_values(s, m_new, a, size):
            p = jnp.exp2(s - m_new).astype(BF16)
            v_ones = jnp.concatenate(
                [v_ref[a:a + size, hcols], jnp.ones((size, V_DIM), BF16)], axis=1)
            return jnp.dot(p, v_ones, preferred_element_type=F32)

        s = jnp.dot(q, kt_ref[krows, 0:CHUNK], preferred_element_type=F32)
        col = lax.broadcasted_iota(jnp.int32, (1, CHUNK), 1)
        s = jnp.where(col >= PAD, s, -jnp.inf)
        m = jnp.max(s, axis=1, keepdims=True)
        acc = weighted_values(s, m, 0, CHUNK)

        for a in range(CHUNK, lp, key_chunk):
            s = jnp.dot(q, kt_ref[krows, a:a + key_chunk], preferred_element_type=F32)
            m_new = jnp.maximum(m, jnp.max(s, axis=1, keepdims=True))
            acc = jnp.exp2(m - m_new) * acc + weighted_values(s, m_new, a, key_chunk)
            m = m_new

        g = gate_ref[:, hcols]
        o_ref[:, hcols] = ((acc[:, :V_DIM] / acc[:, V_DIM:])
                           * (g * jax.nn.sigmoid(g))).astype(BF16)


def _attention(q, kt, v, proj, batch, lp, nq, cfg):
    rows = q.shape[0]
    tq = lp // nq
    assert lp % nq == 0 and tq % 16 == 0
    key_chunk = _pick(lp - CHUNK, ATT_KEY_CHUNK, LANE)
    pair_v = 2 * V_DIM
    gate_b = cfg.off_gate // pair_v
    kern = functools.partial(_attention_kernel, key_chunk=key_chunk)
    return pl.pallas_call(
        kern,
        out_shape=jax.ShapeDtypeStruct((rows, cfg.att_w), BF16),
        grid=(batch, cfg.att_heads // 2, nq),
        in_specs=[
            pl.BlockSpec((tq, 2 * QK_NOPE + LANE), lambda b, h, i: (b * nq + i, h)),
            pl.BlockSpec((4 * QK_NOPE, lp), lambda b, h, i: (h, b)),
            pl.BlockSpec((lp, pair_v), lambda b, h, i: (b, h)),
            pl.BlockSpec((tq, pair_v), lambda b, h, i: (b * nq + i, gate_b + h))],
        out_specs=pl.BlockSpec((tq, pair_v), lambda b, h, i: (b * nq + i, h)),
        compiler_params=_params(3),
        name="attention",
    )(q, kt, v, proj)


def _conv_kernel(x_ref, prev_ref, next_ref, w_ref, b_ref, o_ref, ext_ref):
    i = pl.program_id(0)
    tm = x_ref.shape[0]
    ext_ref[0:SUBLANE, :] = jnp.where(i == 0, 0.0, prev_ref[...])
    ext_ref[SUBLANE:SUBLANE + tm, :] = x_ref[...]
    ext_ref[SUBLANE + tm:2 * SUBLANE + tm, :] = jnp.where(
        i == pl.num_programs(0) - 1, 0.0, next_ref[...])
    ext = ext_ref[...]
    n = ext.shape[0]
    acc = b_ref[...]
    for tap in range(CONV_WIDTH):
        shifted = ext if tap == CONV_HALF else pltpu.roll(ext, (CONV_HALF - tap) % n, 0)
        acc = acc + shifted[SUBLANE:SUBLANE + tm, :] * w_ref[tap:tap + 1, :]
    o_ref[...] = acc * jax.nn.sigmoid(acc)


def _conv(proj, w, b, cfg):
    rows = proj.shape[0]
    tm = _pick(rows, 512, LANE)
    tc = _pick(cfg.xbc_w, 512, LANE)
    assert cfg.off_xbc % tc == 0
    cb = cfg.off_xbc // tc
    hb = tm // SUBLANE
    last = rows // SUBLANE - 1
    return pl.pallas_call(
        _conv_kernel,
        out_shape=jax.ShapeDtypeStruct((rows, cfg.xbc_w), F32),
        grid=(rows // tm, cfg.xbc_w // tc),
        in_specs=[pl.BlockSpec((tm, tc), lambda i, j: (i, cb + j)),
                  pl.BlockSpec((SUBLANE, tc), lambda i, j: (jnp.maximum(i * hb - 1, 0), cb + j)),
                  pl.BlockSpec((SUBLANE, tc),
                               lambda i, j: (jnp.minimum((i + 1) * hb, last), cb + j)),
                  pl.BlockSpec((CONV_WIDTH, tc), lambda i, j: (0, j)),
                  pl.BlockSpec((1, tc), lambda i, j: (0, j))],
        out_specs=pl.BlockSpec((tm, tc), lambda i, j: (i, j)),
        scratch_shapes=[pltpu.VMEM((tm + 2 * SUBLANE, tc), F32)],
        compiler_params=_params(2),
        name="conv_silu",
    )(proj, proj, proj, w, b)


def _ssd_prep_kernel(raw_ref, bias_ref, alog_ref, cum_ref, cumt_ref, dtt_ref, wt_ref, etot_ref):
    x = raw_ref[...] + bias_ref[...]
    dt = jnp.maximum(x, 0.0) + jnp.log1p(jnp.exp(-jnp.abs(x)))
    row = lax.broadcasted_iota(jnp.int32, (CHUNK, DT_LANES), 0)
    lane = lax.broadcasted_iota(jnp.int32, (CHUNK, DT_LANES), 1)
    dt = jnp.where(jnp.logical_and(pl.program_id(1) == 0, row < PAD), 0.0, dt)
    da = dt * (-jnp.exp(alog_ref[...]))
    fwd = da
    bwd = da
    k = 1
    while k < CHUNK:
        fwd = fwd + jnp.where(row >= k, pltpu.roll(fwd, k, 0), 0.0)
        bwd = bwd + jnp.where(row < CHUNK - k, pltpu.roll(bwd, CHUNK - k, 0), 0.0)
        k *= 2
    is_fwd = lane < DT_BWD
    cum = jnp.where(is_fwd, fwd, bwd)
    tot = jnp.where(is_fwd[0:1, :], fwd[CHUNK - 1:CHUNK, :], bwd[0:1, :])
    w = jnp.exp(tot - cum) * dt
    cum = cum * LOG2_E
    cum_ref[...] = cum
    cumt_ref[...] = cum.T
    dtt_ref[...] = dt.T
    wt_ref[...] = w.T
    etot_ref[...] = jnp.broadcast_to(jnp.exp(tot), (CHUNK, DT_LANES)).T


def _ssd_prep(proj, bias, alog, batch, lp, cfg):
    rows = proj.shape[0]
    nc = lp // CHUNK
    cb = cfg.off_dt // DT_LANES
    row_major = pl.BlockSpec((CHUNK, DT_LANES), lambda b, c: (b * nc + c, 0))
    col_major = pl.BlockSpec((DT_LANES, CHUNK), lambda b, c: (0, b * nc + c))
    return pl.pallas_call(
        _ssd_prep_kernel,
        out_shape=(jax.ShapeDtypeStruct((rows, DT_LANES), F32),
                   jax.ShapeDtypeStruct((DT_LANES, rows), F32),
                   jax.ShapeDtypeStruct((DT_LANES, rows), F32),
                   jax.ShapeDtypeStruct((DT_LANES, rows), F32),
                   jax.ShapeDtypeStruct((rows, DT_LANES), F32)),
        grid=(batch, nc),
        in_specs=[pl.BlockSpec((CHUNK, DT_LANES), lambda b, c: (b * nc + c, cb)),
                  pl.BlockSpec((1, DT_LANES), lambda b, c: (0, 0)),
                  pl.BlockSpec((1, DT_LANES), lambda b, c: (0, 0))],
        out_specs=(row_major, col_major, col_major, col_major, row_major),
        compiler_params=_params(2),
        name="ssd_prep",
    )(proj, bias, alog)


def _ssd_direction(reverse, group, head_base, xs_ref, bm_ref, cm_ref, cum_ref, cumt_ref,
                   dtt_ref, wt_ref, etot_ref, d_ref, y_ref, state_ref):
    gcols = slice(group * SSM_STATE, (group + 1) * SSM_STATE)
    bm = bm_ref[:, gcols]
    cm = cm_ref[:, gcols]
    bm_t = bm.T.astype(BF16)
    cm = cm.astype(BF16)
    cb = jnp.dot(cm, bm_t, preferred_element_type=F32).astype(BF16)
    row = lax.broadcasted_iota(jnp.int32, (CHUNK, CHUNK), 0)
    col = lax.broadcasted_iota(jnp.int32, (CHUNK, CHUNK), 1)
    live = (col >= row) if reverse else (row >= col)
    lane = lax.broadcasted_iota(jnp.int32, (CHUNK, DT_LANES), 1)
    low_half = lax.broadcasted_iota(jnp.int32, (1, LANE), 1) < SSM_HEADDIM
    cum = cum_ref[...]
    for pair in range(GROUP_HEADS // 2):
        cols = slice((group * GROUP_HEADS // 2 + pair) * LANE,
                     (group * GROUP_HEADS // 2 + pair + 1) * LANE)
        xs_f32 = xs_ref[:, cols]
        xs_pair = xs_f32.astype(BF16)
        state = state_ref[group * GROUP_HEADS // 2 + pair]
        rhs = jnp.concatenate([xs_pair, state.astype(BF16)], axis=0)
        xs_split = jnp.concatenate([jnp.where(low_half, xs_pair, jnp.zeros_like(xs_pair)),
                                    jnp.where(low_half, jnp.zeros_like(xs_pair), xs_pair)], axis=0)
        ys, wbs, ets = [], [], []
        for t in range(2):
            hd = head_base + 2 * pair + t
            a_col = jnp.sum(jnp.where(lane == hd, cum, 0.0), axis=1, keepdims=True)
            a_row = cumt_ref[pl.ds(hd, 1), :]
            decay = jnp.exp2(jnp.where(live, a_col - a_row, -jnp.inf)).astype(BF16)
            mix = cb * decay * dtt_ref[pl.ds(hd, 1), :].astype(BF16)
            carry_in = cm * jnp.broadcast_to(jnp.exp2(a_col), (CHUNK, SSM_STATE)).astype(BF16)
            lhs = jnp.concatenate([mix, carry_in], axis=1)
            ys.append(jnp.dot(lhs, rhs, preferred_element_type=F32))
            wbs.append(bm_t * wt_ref[pl.ds(hd, 1), :].astype(BF16))
            ets.append(etot_ref[pl.ds(hd, 1), :])
        y = jnp.where(low_half, ys[0], ys[1])
        if d_ref is not None:
            y = y + xs_f32 * d_ref[:, cols]
        y_ref[:, cols] = y
        new_state = jnp.dot(jnp.concatenate(wbs, axis=1), xs_split, preferred_element_type=F32)
        state_ref[group * GROUP_HEADS // 2 + pair] = (
            state * jnp.where(low_half, ets[0], ets[1]) + new_state)


def _ssd_kernel(*refs, groups_per_step):
    fwd_in, bwd_in = refs[0:8], refs[8:16]
    d_ref, yf_ref, yb_ref, state_ref = refs[16:20]

    @pl.when(pl.program_id(2) == 0)
    def _():
        state_ref[...] = jnp.zeros_like(state_ref)

    for group in range(groups_per_step):
        head_base = (pl.program_id(1) * groups_per_step + group) * GROUP_HEADS
        _ssd_direction(False, group, head_base, *fwd_in, d_ref, yf_ref, state_ref.at[0])
        _ssd_direction(True, group, DT_BWD + head_base, *bwd_in, None, yb_ref, state_ref.at[1])


def _ssd(xbc, cum, cum_t, dt_t, w_t, etot, d_exp, batch, lp, cfg):
    rows = xbc.shape[0]
    nc = lp // CHUNK
    gps = min(SSD_GROUPS_PER_STEP, cfg.ssm_groups)
    assert cfg.ssm_groups % gps == 0 and (cfg.ssm_w // SSM_STATE) % gps == 0
    gw = gps * GROUP_HEADS * SSM_HEADDIM
    sw = gps * SSM_STATE
    bm_b = cfg.ssm_w // sw
    cm_b = bm_b + cfg.ssm_groups // gps

    def specs(chunk_of):
        r = lambda b, c: b * nc + chunk_of(c)
        return [pl.BlockSpec((CHUNK, gw), lambda b, g, c: (r(b, c), g)),
                pl.BlockSpec((CHUNK, sw), lambda b, g, c: (r(b, c), bm_b + g)),
                pl.BlockSpec((CHUNK, sw), lambda b, g, c: (r(b, c), cm_b + g)),
                pl.BlockSpec((CHUNK, DT_LANES), lambda b, g, c: (r(b, c), 0)),
                pl.BlockSpec((DT_LANES, CHUNK), lambda b, g, c: (0, r(b, c))),
                pl.BlockSpec((DT_LANES, CHUNK), lambda b, g, c: (0, r(b, c))),
                pl.BlockSpec((DT_LANES, CHUNK), lambda b, g, c: (0, r(b, c))),
                pl.BlockSpec((CHUNK, DT_LANES), lambda b, g, c: (r(b, c), 0))]

    fwd_chunk = lambda c: c
    bwd_chunk = lambda c: nc - 1 - c
    args = (xbc, xbc, xbc, cum, cum_t, dt_t, w_t, etot)
    return pl.pallas_call(
        functools.partial(_ssd_kernel, groups_per_step=gps),
        out_shape=(jax.ShapeDtypeStruct((rows, cfg.ssm_w), F32),
                   jax.ShapeDtypeStruct((rows, cfg.ssm_w), F32)),
        grid=(batch, cfg.ssm_groups // gps, nc),
        in_specs=specs(fwd_chunk) + specs(bwd_chunk) + [
            pl.BlockSpec((1, gw), lambda b, g, c: (0, g))],
        out_specs=(pl.BlockSpec((CHUNK, gw), lambda b, g, c: (b * nc + c, g)),
                   pl.BlockSpec((CHUNK, gw), lambda b, g, c: (b * nc + nc - 1 - c, g))),
        scratch_shapes=[pltpu.VMEM((2, gps * GROUP_HEADS // 2, SSM_STATE, LANE), F32)],
        compiler_params=_params(3),
        name="ssd_scan",
    )(*args, *args, d_exp)


def _ssd_finish_kernel(yf_ref, yb_ref, z_ref, g_ref, o_ref, *, row_chunk):
    rows, width = o_ref.shape

    def body(r, carry):
        sl = pl.ds(pl.multiple_of(r * row_chunk, row_chunk), row_chunk)
        z = z_ref[sl, :]
        y = (yf_ref[sl, :] + yb_ref[sl, :]) * (z * jax.nn.sigmoid(z))
        ms = jnp.sum(y * y, axis=-1, keepdims=True) / width
        o_ref[sl, :] = (y * lax.rsqrt(ms + NORM_EPS) * g_ref[...]).astype(BF16)
        return carry

    lax.fori_loop(0, rows // row_chunk, body, 0)


def _ssd_finish(yf, yb, proj, gain, cfg):
    rows = yf.shape[0]
    tm = _pick(rows, 256, LANE)
    w = cfg.ssm_w
    row_spec = pl.BlockSpec((tm, w), lambda i: (i, 0))
    vec_spec = pl.BlockSpec((1, w), lambda i: (0, 0))
    return pl.pallas_call(
        functools.partial(_ssd_finish_kernel, row_chunk=32),
        out_shape=jax.ShapeDtypeStruct((rows, w), BF16),
        grid=(rows // tm,),
        in_specs=[row_spec, row_spec, row_spec, vec_spec],
        out_specs=row_spec,
        compiler_params=_params(1),
        name="ssd_finish",
    )(yf, yb, proj, gain)


def _out_proj_kernel(oa_ref, os_ref, wa_ref, ws_ref, h_ref, o_ref):
    acc = jnp.dot(oa_ref[...], wa_ref[...], preferred_element_type=F32)
    acc = acc + jnp.dot(os_ref[...], ws_ref[...], preferred_element_type=F32)
    o_ref[...] = h_ref[...] + acc


def _out_proj(o_att, o_ssm, w, h, cfg):
    rows, d = h.shape
    tm = _pick(rows, 512, LANE)
    tn = _pick(d, 512, LANE)
    kw = cfg.att_w
    return pl.pallas_call(
        _out_proj_kernel,
        out_shape=jax.ShapeDtypeStruct((rows, d), F32),
        grid=(rows // tm, d // tn),
        in_specs=[pl.BlockSpec((tm, kw), lambda i, j: (i, 0)),
                  pl.BlockSpec((tm, kw), lambda i, j: (i, 0)),
                  pl.BlockSpec((kw, tn), lambda i, j: (0, j)),
                  pl.BlockSpec((kw, tn), lambda i, j: (1, j)),
                  pl.BlockSpec((tm, tn), lambda i, j: (i, j))],
        out_specs=pl.BlockSpec((tm, tn), lambda i, j: (i, j)),
        input_output_aliases={4: 0},
        compiler_params=_params(2),
        name="out_proj",
    )(o_att, o_ssm, w, w, h)


def _final_norm_kernel(x_ref, g_ref, o_ref, *, row_chunk):
    rows, width = o_ref.shape

    def body(r, carry):
        sl = pl.ds(pl.multiple_of(r * row_chunk, row_chunk), row_chunk)
        x = x_ref[sl, :]
        ms = jnp.sum(x * x, axis=-1, keepdims=True) / width
        o_ref[sl, :] = x * lax.rsqrt(ms + NORM_EPS) * g_ref[...]
        return carry

    lax.fori_loop(0, rows // row_chunk, body, 0)


def _final_norm(h, gain, batch, lp):
    d = h.shape[1]
    nc = lp // CHUNK
    seq = lp - CHUNK
    return pl.pallas_call(
        functools.partial(_final_norm_kernel, row_chunk=32),
        out_shape=jax.ShapeDtypeStruct((batch * seq, d), F32),
        grid=(batch, nc - 1),
        in_specs=[pl.BlockSpec((CHUNK, d), lambda b, c: (b * nc + 1 + c, 0)),
                  pl.BlockSpec((1, d), lambda b, c: (0, 0))],
        out_specs=pl.BlockSpec((CHUNK, d), lambda b, c: (b * (nc - 1) + c, 0)),
        compiler_params=_params(2),
        name="final_norm",
    )(h, gain)


def _prep_layer_weights(cfg, w_in, w_q_up, w_kv_up, w_out, dt_bias, a_log, d_skip):
    q, kv, hs = cfg.q_lora, cfg.kv_lora, cfg.ssm_heads
    d = cfg.d_model
    o = 0
    w_q = w_in[:, o:o + q]; o += q
    w_kv = w_in[:, o:o + kv]; o += kv
    w_kr = w_in[:, o:o + QK_ROPE]; o += QK_ROPE
    w_gate = w_in[:, o:o + cfg.att_w]; o += cfg.att_w
    w_z = w_in[:, o:o + cfg.ssm_w]; o += cfg.ssm_w
    w_xbc = w_in[:, o:o + cfg.xbc_w]; o += cfg.xbc_w
    w_dt = w_in[:, o:o + 2 * hs]

    def pad_dirs(t):
        z = jnp.zeros(t.shape[:-1] + (DT_BWD - hs,), t.dtype)
        return jnp.concatenate([t[..., :hs], z, t[..., hs:], z], axis=-1)

    w_in_p = jnp.concatenate(
        [w_z, w_gate, w_xbc, w_kv, w_q, pad_dirs(w_dt), w_kr,
         jnp.zeros((d, LANE - QK_ROPE), w_in.dtype)], axis=1).astype(BF16)

    wq = w_q_up.reshape(q, cfg.att_heads // 2, 2, QK_NOPE + QK_ROPE)
    wq_p = jnp.concatenate(
        [wq[:, :, 0, :QK_NOPE], wq[:, :, 1, :QK_NOPE], wq[:, :, 0, QK_NOPE:], wq[:, :, 1, QK_NOPE:]],
        axis=-1).reshape(q, cfg.q_w).astype(BF16)

    wkv = w_kv_up.reshape(kv, cfg.att_heads, QK_NOPE + V_DIM)
    wk_t = wkv[:, :, :QK_NOPE].reshape(kv, cfg.att_heads * QK_NOPE).T.astype(BF16)
    wv = wkv[:, :, QK_NOPE:].reshape(kv, cfg.att_w).astype(BF16)

    return dict(
        w_in=w_in_p, w_q=wq_p, wk_t=wk_t, wv=wv, w_out=w_out.astype(BF16),
        dt_bias=pad_dirs(dt_bias.reshape(1, 2 * hs)),
        a_log=pad_dirs(a_log.reshape(1, 2 * hs)),
        d_exp=jnp.repeat(d_skip, SSM_HEADDIM).reshape(1, cfg.ssm_w))


def _run_trunk(x, meta_tokens, layers, final_norm, nq, cfg):
    batch, seq, d = x.shape
    assert seq % ATT_KEY_CHUNK == 0 or seq < ATT_KEY_CHUNK
    lp = seq + CHUNK
    h = jnp.concatenate(
        [jnp.zeros((batch, PAD, d), x.dtype),
         jnp.broadcast_to(meta_tokens[None].astype(x.dtype), (batch, N_META, d)), x],
        axis=1).reshape(batch * lp, d)
    cos_t, sin_t = _rope_tables(batch, lp)
    for lw in layers:
        proj = _in_proj(h, lw["norm_mix"], lw["w_in"], cfg)
        q = _q_up(proj, lw["q_norm"], lw["w_q"], cos_t, sin_t, cfg)
        kr_t = _k_rope(proj, cos_t, sin_t, cfg)
        kt, v = _kv_up(proj, lw["kv_norm"], lw["wk_t"], lw["wv"], kr_t, cfg)
        o_att = _attention(q, kt, v, proj, batch, lp, nq, cfg)
        xbc = _conv(proj, lw["conv_w"], lw["conv_b"], cfg)
        cum, cum_t, dt_t, w_t, etot = _ssd_prep(proj, lw["dt_bias"], lw["a_log"], batch, lp, cfg)
        yf, yb = _ssd(xbc, cum, cum_t, dt_t, w_t, etot, lw["d_exp"], batch, lp, cfg)
        o_ssm = _ssd_finish(yf, yb, proj, lw["ssm_norm"], cfg)
        h = _out_proj(o_att, o_ssm, lw["w_out"], h, cfg)
    y = _final_norm(h, final_norm.reshape(1, d), batch, lp)
    return y.reshape(batch, seq, d)


def _query_blocks(lp):
    for nq in range(1, lp // 16 + 1):
        if lp % nq == 0 and (lp // nq) % 16 == 0 and lp // nq <= 576:
            return nq
    raise ValueError(lp)


def _forward(cfg, x_prompt, x_sample, meta_tokens, norm_mix, w_in, q_norm, w_q_up, kv_norm,
             w_kv_up, conv_w, conv_b, dt_bias, a_log, d_skip, ssm_norm, w_out, final_norm):
    _check_cfg(cfg)
    depth = w_in.shape[0]
    layers = []
    for i in range(depth):
        lw = _prep_layer_weights(cfg, w_in[i], w_q_up[i], w_kv_up[i], w_out[i],
                                 dt_bias[i], a_log[i], d_skip[i])
        lw.update(norm_mix=norm_mix[i].reshape(1, -1), q_norm=q_norm[i].reshape(1, -1),
                  kv_norm=kv_norm[i].reshape(1, -1), conv_w=conv_w[i],
                  conv_b=conv_b[i].reshape(1, -1), ssm_norm=ssm_norm[i].reshape(1, -1))
        layers.append(lw)
    outs = []
    for x in (x_prompt, x_sample):
        nq = _query_blocks(x.shape[1] + CHUNK)
        outs.append(_run_trunk(x, meta_tokens, layers, final_norm, nq, cfg))
    return tuple(outs)


def kernel(x_prompt, x_sample, meta_tokens, norm_mix, w_in, q_norm, w_q_up, kv_norm, w_kv_up,
           conv_w, conv_b, dt_bias, a_log, d_skip, ssm_norm, w_out, final_norm):
    return _forward(Cfg(), x_prompt, x_sample, meta_tokens, norm_mix, w_in, q_norm, w_q_up,
                    kv_norm, w_kv_up, conv_w, conv_b, dt_bias, a_log, d_skip, ssm_norm, w_out,
                    final_norm)
```

```python
import functools
import math
from typing import NamedTuple

import jax
import jax.numpy as jnp
from jax import lax
from jax.experimental import pallas as pl
from jax.experimental.pallas import tpu as pltpu

F32 = jnp.float32
BF16 = jnp.bfloat16

N_META = 16
NORM_EPS = 1e-6
QK_NOPE = 128
QK_ROPE = 64
V_DIM = 128
ROPE_THETA = 10000.0
SSM_HEADDIM = 64
SSM_STATE = 128
CONV_WIDTH = 5
CONV_HALF = CONV_WIDTH // 2
CHUNK = 128
PAD = CHUNK - N_META
GROUP_HEADS = 8
DT_LANES = 128
DT_BWD = 64
LANE = 128
SUBLANE = 8
MXU_COLS = 256
LOG2_E = math.log2(math.e)
Q_PRESCALE = LOG2_E / math.sqrt(QK_NOPE + QK_ROPE)
VMEM_LIMIT = 56 * 1024 * 1024
ATT_KEY_CHUNK = 512
ATT_MAX_QUERIES = 2304
ONES_ROWS = 16
SSD_GROUPS_PER_STEP = 4


class Cfg(NamedTuple):
    d_model: int = 4096
    att_heads: int = 32
    q_lora: int = 1536
    kv_lora: int = 512
    ssm_heads: int = 64
    ssm_groups: int = 8

    @property
    def att_w(self):
        return self.att_heads * V_DIM

    @property
    def ssm_w(self):
        return self.ssm_heads * SSM_HEADDIM

    @property
    def gn(self):
        return self.ssm_groups * SSM_STATE

    @property
    def xbc_w(self):
        return self.ssm_w + 2 * self.gn

    @property
    def q_piece(self):
        return self.q_lora // 3

    @property
    def off_z(self):
        return 0

    @property
    def off_gate(self):
        return self.ssm_w

    @property
    def off_xbc(self):
        return self.off_gate + self.att_w

    @property
    def off_kv(self):
        return self.off_xbc + self.xbc_w

    @property
    def off_q(self):
        return self.off_kv + self.kv_lora

    @property
    def off_dt(self):
        return self.off_q + self.q_lora

    @property
    def off_kr(self):
        return self.off_dt + DT_LANES

    @property
    def proj_w(self):
        return self.off_kr + LANE

    @property
    def q_w(self):
        return (self.att_heads // 2) * 384


def _check_cfg(cfg):
    assert cfg.ssm_w == cfg.d_model == cfg.att_w
    assert cfg.ssm_heads == cfg.ssm_groups * GROUP_HEADS and cfg.ssm_heads <= DT_BWD
    assert cfg.att_heads % 2 == 0 and cfg.q_lora % 3 == 0 and cfg.q_piece % LANE == 0
    assert cfg.kv_lora % LANE == 0
    assert cfg.off_kv % cfg.kv_lora == 0 and cfg.off_q % cfg.q_piece == 0


def _pick(n, target, unit):
    best = None
    for t in range(unit, min(n, target) + 1, unit):
        if n % t == 0:
            best = t
    assert best is not None, (n, target, unit)
    return best


def _params(n_axes):
    return pltpu.CompilerParams(dimension_semantics=("arbitrary",) * n_axes,
                                vmem_limit_bytes=VMEM_LIMIT)


def _rmsnorm_rows(dst_ref, piece_refs, gain_refs, rows, row_chunk=64):
    width = sum(p.shape[1] for p in piece_refs)
    row_chunk = min(row_chunk, rows)

    def body(r, carry):
        sl = pl.ds(pl.multiple_of(r * row_chunk, row_chunk), row_chunk)
        xs = [p[sl, :] for p in piece_refs]
        ss = xs[0] * xs[0]
        ss = jnp.sum(ss, axis=-1, keepdims=True)
        for x in xs[1:]:
            ss = ss + jnp.sum(x * x, axis=-1, keepdims=True)
        rstd = lax.rsqrt(ss / width + NORM_EPS)
        off = 0
        for x, g in zip(xs, gain_refs):
            w = x.shape[1]
            dst_ref[sl, off:off + w] = (x * rstd * g[...]).astype(BF16)
            off += w
        return carry

    lax.fori_loop(0, rows // row_chunk, body, 0)


def _in_proj_kernel(x_ref, g_ref, w_ref, o_ref, u_ref):
    @pl.when(pl.program_id(1) == 0)
    def _():
        _rmsnorm_rows(u_ref, [x_ref], [g_ref], x_ref.shape[0])

    o_ref[...] = jnp.dot(u_ref[...], w_ref[...], preferred_element_type=F32)


def _in_proj(h, gain, w, cfg):
    rows, d = h.shape
    n = w.shape[1]
    tm = _pick(rows, 512, LANE)
    tn = _pick(n, 1280, MXU_COLS)
    return pl.pallas_call(
        _in_proj_kernel,
        out_shape=jax.ShapeDtypeStruct((rows, n), F32),
        grid=(rows // tm, n // tn),
        in_specs=[pl.BlockSpec((tm, d), lambda i, j: (i, 0)),
                  pl.BlockSpec((1, d), lambda i, j: (0, 0)),
                  pl.BlockSpec((d, tn), lambda i, j: (0, j))],
        out_specs=pl.BlockSpec((tm, tn), lambda i, j: (i, j)),
        scratch_shapes=[pltpu.VMEM((tm, d), BF16)],
        compiler_params=_params(2),
        name="in_proj",
    )(h, gain, w)


def _rope128(r, cos_t, sin_t):
    lane = lax.broadcasted_iota(jnp.int32, (1, LANE), 1)
    first_half = (lane % QK_ROPE) < (QK_ROPE // 2)
    partner = jnp.where(first_half,
                        pltpu.roll(r, LANE - QK_ROPE // 2, 1),
                        pltpu.roll(r, QK_ROPE // 2, 1))
    return r * cos_t + partner * sin_t


def _rope_tables(batch, lp):
    half = QK_ROPE // 2
    inv = 1.0 / (ROPE_THETA ** (jnp.arange(half, dtype=F32) / half))
    pos = jnp.maximum(jnp.arange(lp, dtype=F32) - PAD, 0.0)
    ang = pos[:, None] * inv[None, :]
    cos, sin = jnp.cos(ang), jnp.sin(ang)
    cos_t = jnp.tile(cos, (batch, LANE // half))
    sin_t = jnp.tile(jnp.concatenate([-sin, sin], axis=1), (batch, LANE // QK_ROPE))
    return cos_t, sin_t


def _q_up_kernel(x0_ref, x1_ref, x2_ref, g0_ref, g1_ref, g2_ref, w_ref, cos_ref, sin_ref,
                 o_ref, u_ref):
    @pl.when(pl.program_id(1) == 0)
    def _():
        _rmsnorm_rows(u_ref, [x0_ref, x1_ref, x2_ref], [g0_ref, g1_ref, g2_ref],
                      x0_ref.shape[0])

    acc = jnp.dot(u_ref[...], w_ref[...], preferred_element_type=F32) * Q_PRESCALE
    pair_w = 2 * QK_NOPE + LANE
    for p in range(acc.shape[1] // pair_w):
        base = p * pair_w
        o_ref[:, base:base + 2 * QK_NOPE] = acc[:, base:base + 2 * QK_NOPE].astype(BF16)
        r = acc[:, base + 2 * QK_NOPE:base + pair_w]
        o_ref[:, base + 2 * QK_NOPE:base + pair_w] = _rope128(
            r, cos_ref[...], sin_ref[...]).astype(BF16)


def _q_up(proj, gain, w, cos_t, sin_t, cfg):
    rows = proj.shape[0]
    tm = _pick(rows, 512, LANE)
    qp = cfg.q_piece
    tn = _pick(cfg.q_w, 1536, 768)
    pb = cfg.off_q // qp
    x_specs = [pl.BlockSpec((tm, qp), functools.partial(lambda i, j, k: (i, pb + k), k=k))
               for k in range(3)]
    g_specs = [pl.BlockSpec((1, qp), functools.partial(lambda i, j, k: (0, k), k=k))
               for k in range(3)]
    return pl.pallas_call(
        _q_up_kernel,
        out_shape=jax.ShapeDtypeStruct((rows, cfg.q_w), BF16),
        grid=(rows // tm, cfg.q_w // tn),
        in_specs=x_specs + g_specs + [
            pl.BlockSpec((cfg.q_lora, tn), lambda i, j: (0, j)),
            pl.BlockSpec((tm, LANE), lambda i, j: (i, 0)),
            pl.BlockSpec((tm, LANE), lambda i, j: (i, 0))],
        out_specs=pl.BlockSpec((tm, tn), lambda i, j: (i, j)),
        scratch_shapes=[pltpu.VMEM((tm, cfg.q_lora), BF16)],
        compiler_params=_params(2),
        name="q_up",
    )(proj, proj, proj, gain, gain, gain, w, cos_t, sin_t)


def _k_rope_kernel(x_ref, cos_ref, sin_ref, o_ref):
    r = _rope128(x_ref[...], cos_ref[...], sin_ref[...])
    o_ref[...] = (r + pltpu.roll(r, QK_ROPE, 1)).astype(BF16)


def _k_rope(proj, cos_t, sin_t, cfg):
    rows = proj.shape[0]
    tm = _pick(rows, 512, LANE)
    cb = cfg.off_kr // LANE
    return pl.pallas_call(
        _k_rope_kernel,
        out_shape=jax.ShapeDtypeStruct((rows, LANE), BF16),
        grid=(rows // tm,),
        in_specs=[pl.BlockSpec((tm, LANE), lambda i: (i, cb)),
                  pl.BlockSpec((tm, LANE), lambda i: (i, 0)),
                  pl.BlockSpec((tm, LANE), lambda i: (i, 0))],
        out_specs=pl.BlockSpec((tm, LANE), lambda i: (i, 0)),
        compiler_params=_params(1),
        name="k_rope",
    )(proj, cos_t, sin_t)


def _kv_up_kernel(x_ref, g_ref, wk_ref, wvt_ref, kr_ref, k_ref, vt_ref, u_ref):
    @pl.when(pl.program_id(1) == 0)
    def _():
        _rmsnorm_rows(u_ref, [x_ref], [g_ref], x_ref.shape[0])

    u = u_ref[...]
    kn = jnp.dot(u, wk_ref[...], preferred_element_type=F32)
    kr2 = kr_ref[...]
    slot_a_lanes = lax.broadcasted_iota(jnp.int32, (1, LANE), 1) < QK_ROPE
    zero = jnp.zeros_like(kr2)
    rope_cols = (jnp.where(slot_a_lanes, kr2, zero), jnp.where(slot_a_lanes, zero, kr2))
    heads = wk_ref.shape[1] // QK_NOPE
    for t in range(heads):
        base = t * 2 * QK_NOPE
        k_ref[:, base:base + QK_NOPE] = kn[:, t * QK_NOPE:(t + 1) * QK_NOPE].astype(BF16)
        k_ref[:, base + QK_NOPE:base + 2 * QK_NOPE] = rope_cols[t % 2]
    vt_ref[...] = lax.dot_general(wvt_ref[...], u, (((1,), (1,)), ((), ())),
                                  preferred_element_type=F32).astype(BF16)


def _kv_up(proj, gain, wk, wv_t, kr2, cfg):
    rows = proj.shape[0]
    tm = _pick(rows, 512, LANE)
    hb = 8 if cfg.att_heads % 8 == 0 else 2
    kvb = cfg.off_kv // cfg.kv_lora
    return pl.pallas_call(
        _kv_up_kernel,
        out_shape=(jax.ShapeDtypeStruct((rows, cfg.att_heads * 2 * QK_NOPE), BF16),
                   jax.ShapeDtypeStruct((cfg.att_w, rows), BF16)),
        grid=(rows // tm, cfg.att_heads // hb),
        in_specs=[pl.BlockSpec((tm, cfg.kv_lora), lambda i, j: (i, kvb)),
                  pl.BlockSpec((1, cfg.kv_lora), lambda i, j: (0, 0)),
                  pl.BlockSpec((cfg.kv_lora, hb * QK_NOPE), lambda i, j: (0, j)),
                  pl.BlockSpec((hb * V_DIM, cfg.kv_lora), lambda i, j: (j, 0)),
                  pl.BlockSpec((tm, LANE), lambda i, j: (i, 0))],
        out_specs=(pl.BlockSpec((tm, hb * 2 * QK_NOPE), lambda i, j: (i, j)),
                   pl.BlockSpec((hb * V_DIM, tm), lambda i, j: (j, i))),
        scratch_shapes=[pltpu.VMEM((tm, cfg.kv_lora), BF16)],
        compiler_params=_params(2),
        name="kv_up",
    )(proj, gain, wk, wv_t, kr2)


def _attention_kernel(q_ref, k_ref, vt_ref, gate_ref, o_ref, *, key_chunk):
    lp = k_ref.shape[0]
    q_rope = q_ref[:, 2 * QK_NOPE:]
    nt_dims = (((1,), (1,)), ((), ()))
    for t in range(2):
        hcols = slice(t * V_DIM, (t + 1) * V_DIM)
        kcols = slice(t * 2 * QK_NOPE, (t + 1) * 2 * QK_NOPE)
        q = jnp.concatenate([q_ref[:, hcols], q_rope], axis=1)

        def weighted_values(s_t, m_new, a, size):
            p_t = jnp.exp2(s_t - m_new).astype(BF16)
            vt_ones = jnp.concatenate(
                [vt_ref[hcols, a:a + size], jnp.ones((ONES_ROWS, size), BF16)], axis=0)
            return jnp.dot(vt_ones, p_t, preferred_element_type=F32)

        s_t = lax.dot_general(k_ref[0:CHUNK, kcols], q, nt_dims, preferred_element_type=F32)
        key = lax.broadcasted_iota(jnp.int32, (CHUNK, 1), 0)
        s_t = jnp.where(key >= PAD, s_t, -jnp.inf)
        m = jnp.max(s_t, axis=0, keepdims=True)
        acc = weighted_values(s_t, m, 0, CHUNK)

        for a in range(CHUNK, lp, key_chunk):
            s_t = lax.dot_general(k_ref[a:a + key_chunk, kcols], q, nt_dims,
                                  preferred_element_type=F32)
            m_new = jnp.maximum(m, jnp.max(s_t, axis=0, keepdims=True))
            acc = jnp.exp2(m - m_new) * acc + weighted_values(s_t, m_new, a, key_chunk)
            m = m_new

        out = (acc[:V_DIM, :] / acc[V_DIM:V_DIM + 1, :]).T
        g = gate_ref[:, hcols]
        o_ref[:, hcols] = (out * (g * jax.nn.sigmoid(g))).astype(BF16)


def _attention(q, k, vt, proj, batch, lp, nq, cfg):
    rows = q.shape[0]
    tq = lp // nq
    assert lp % nq == 0 and tq % LANE == 0
    key_chunk = _pick(lp - CHUNK, ATT_KEY_CHUNK, LANE)
    pair_v = 2 * V_DIM
    gate_b = cfg.off_gate // pair_v
    kern = functools.partial(_attention_kernel, key_chunk=key_chunk)
    return pl.pallas_call(
        kern,
        out_shape=jax.ShapeDtypeStruct((rows, cfg.att_w), BF16),
        grid=(batch, cfg.att_heads // 2, nq),
        in_specs=[
            pl.BlockSpec((tq, 2 * QK_NOPE + LANE), lambda b, h, i: (b * nq + i, h)),
            pl.BlockSpec((lp, 4 * QK_NOPE), lambda b, h, i: (b, h)),
            pl.BlockSpec((pair_v, lp), lambda b, h, i: (h, b)),
            pl.BlockSpec((tq, pair_v), lambda b, h, i: (b * nq + i, gate_b + h))],
        out_specs=pl.BlockSpec((tq, pair_v), lambda b, h, i: (b * nq + i, h)),
        compiler_params=_params(3),
        name="attention",
    )(q, k, vt, proj)


def _conv_kernel(x_ref, prev_ref, next_ref, w_ref, b_ref, o_ref, ext_ref):
    i = pl.program_id(0)
    tm = x_ref.shape[0]
    ext_ref[0:SUBLANE, :] = jnp.where(i == 0, 0.0, prev_ref[...])
    ext_ref[SUBLANE:SUBLANE + tm, :] = x_ref[...]
    ext_ref[SUBLANE + tm:2 * SUBLANE + tm, :] = jnp.where(
        i == pl.num_programs(0) - 1, 0.0, next_ref[...])
    ext = ext_ref[...]
    n = ext.shape[0]
    acc = b_ref[...]
    for tap in range(CONV_WIDTH):
        shifted = ext if tap == CONV_HALF else pltpu.roll(ext, (CONV_HALF - tap) % n, 0)
        acc = acc + shifted[SUBLANE:SUBLANE + tm, :] * w_ref[tap:tap + 1, :]
    o_ref[...] = acc * jax.nn.sigmoid(acc)


def _conv(proj, w, b, cfg):
    rows = proj.shape[0]
    tm = _pick(rows, 512, LANE)
    tc = _pick(math.gcd(cfg.xbc_w, cfg.off_xbc), 2048, LANE)
    cb = cfg.off_xbc // tc
    hb = tm // SUBLANE
    last = rows // SUBLANE - 1
    return pl.pallas_call(
        _conv_kernel,
        out_shape=jax.ShapeDtypeStruct((rows, cfg.xbc_w), F32),
        grid=(rows // tm, cfg.xbc_w // tc),
        in_specs=[pl.BlockSpec((tm, tc), lambda i, j: (i, cb + j)),
                  pl.BlockSpec((SUBLANE, tc), lambda i, j: (jnp.maximum(i * hb - 1, 0), cb + j)),
                  pl.BlockSpec((SUBLANE, tc),
                               lambda i, j: (jnp.minimum((i + 1) * hb, last), cb + j)),
                  pl.BlockSpec((CONV_WIDTH, tc), lambda i, j: (0, j)),
                  pl.BlockSpec((1, tc), lambda i, j: (0, j))],
        out_specs=pl.BlockSpec((tm, tc), lambda i, j: (i, j)),
        scratch_shapes=[pltpu.VMEM((tm + 2 * SUBLANE, tc), F32)],
        compiler_params=_params(2),
        name="conv_silu",
    )(proj, proj, proj, w, b)


def _ssd_prep_kernel(raw_ref, bias_ref, alog_ref, cum_ref, cumt_ref, dtt_ref, wt_ref, etot_ref):
    x = raw_ref[...] + bias_ref[...]
    dt = jnp.maximum(x, 0.0) + jnp.log1p(jnp.exp(-jnp.abs(x)))
    row = lax.broadcasted_iota(jnp.int32, (CHUNK, DT_LANES), 0)
    lane = lax.broadcasted_iota(jnp.int32, (CHUNK, DT_LANES), 1)
    dt = jnp.where(jnp.logical_and(pl.program_id(1) == 0, row < PAD), 0.0, dt)
    da = dt * (-jnp.exp(alog_ref[...]))
    fwd = da
    bwd = da
    k = 1
    while k < CHUNK:
        fwd = fwd + jnp.where(row >= k, pltpu.roll(fwd, k, 0), 0.0)
        bwd = bwd + jnp.where(row < CHUNK - k, pltpu.roll(bwd, CHUNK - k, 0), 0.0)
        k *= 2
    is_fwd = lane < DT_BWD
    cum = jnp.where(is_fwd, fwd, bwd)
    tot = jnp.where(is_fwd[0:1, :], fwd[CHUNK - 1:CHUNK, :], bwd[0:1, :])
    w = jnp.exp(tot - cum) * dt
    cum = cum * LOG2_E
    cum_ref[...] = cum
    cumt_ref[...] = cum.T
    dtt_ref[...] = dt.T
    wt_ref[...] = w.T
    etot_ref[...] = jnp.broadcast_to(jnp.exp(tot), (CHUNK, DT_LANES)).T


def _ssd_prep(proj, bias, alog, batch, lp, cfg):
    rows = proj.shape[0]
    nc = lp // CHUNK
    cb = cfg.off_dt // DT_LANES
    row_major = pl.BlockSpec((CHUNK, DT_LANES), lambda b, c: (b * nc + c, 0))
    col_major = pl.BlockSpec((DT_LANES, CHUNK), lambda b, c: (0, b * nc + c))
    return pl.pallas_call(
        _ssd_prep_kernel,
        out_shape=(jax.ShapeDtypeStruct((rows, DT_LANES), F32),
                   jax.ShapeDtypeStruct((DT_LANES, rows), F32),
                   jax.ShapeDtypeStruct((DT_LANES, rows), F32),
                   jax.ShapeDtypeStruct((DT_LANES, rows), F32),
                   jax.ShapeDtypeStruct((rows, DT_LANES), F32)),
        grid=(batch, nc),
        in_specs=[pl.BlockSpec((CHUNK, DT_LANES), lambda b, c: (b * nc + c, cb)),
                  pl.BlockSpec((1, DT_LANES), lambda b, c: (0, 0)),
                  pl.BlockSpec((1, DT_LANES), lambda b, c: (0, 0))],
        out_specs=(row_major, col_major, col_major, col_major, row_major),
        compiler_params=_params(2),
        name="ssd_prep",
    )(proj, bias, alog)


def _ssd_direction(reverse, group, head_base, xs_ref, bm_ref, cm_ref, cum_ref, cumt_ref,
                   dtt_ref, wt_ref, etot_ref, d_ref, y_ref, state_ref):
    gcols = slice(group * SSM_STATE, (group + 1) * SSM_STATE)
    bm = bm_ref[:, gcols]
    cm = cm_ref[:, gcols]
    bm_t = bm.T.astype(BF16)
    cm = cm.astype(BF16)
    cb = jnp.dot(cm, bm_t, preferred_element_type=F32).astype(BF16)
    row = lax.broadcasted_iota(jnp.int32, (CHUNK, CHUNK), 0)
    col = lax.broadcasted_iota(jnp.int32, (CHUNK, CHUNK), 1)
    live = (col >= row) if reverse else (row >= col)
    lane = lax.broadcasted_iota(jnp.int32, (CHUNK, DT_LANES), 1)
    low_half = lax.broadcasted_iota(jnp.int32, (1, LANE), 1) < SSM_HEADDIM
    cum = cum_ref[...]
    for pair in range(GROUP_HEADS // 2):
        cols = slice((group * GROUP_HEADS // 2 + pair) * LANE,
                     (group * GROUP_HEADS // 2 + pair + 1) * LANE)
        xs_f32 = xs_ref[:, cols]
        xs_pair = xs_f32.astype(BF16)
        state = state_ref[group * GROUP_HEADS // 2 + pair]
        rhs = jnp.concatenate([xs_pair, state.astype(BF16)], axis=0)
        xs_split = jnp.concatenate([jnp.where(low_half, xs_pair, jnp.zeros_like(xs_pair)),
                                    jnp.where(low_half, jnp.zeros_like(xs_pair), xs_pair)], axis=0)
        ys, wbs, ets = [], [], []
        for t in range(2):
            hd = head_base + 2 * pair + t
            a_col = jnp.sum(jnp.where(lane == hd, cum, 0.0), axis=1, keepdims=True)
            a_row = cumt_ref[pl.ds(hd, 1), :]
            decay = jnp.exp2(jnp.where(live, a_col - a_row, -jnp.inf)).astype(BF16)
            mix = cb * decay * dtt_ref[pl.ds(hd, 1), :].astype(BF16)
            carry_in = cm * jnp.broadcast_to(jnp.exp2(a_col), (CHUNK, SSM_STATE)).astype(BF16)
            lhs = jnp.concatenate([mix, carry_in], axis=1)
            ys.append(jnp.dot(lhs, rhs, preferred_element_type=F32))
            wbs.append(bm_t * wt_ref[pl.ds(hd, 1), :].astype(BF16))
            ets.append(etot_ref[pl.ds(hd, 1), :])
        y = jnp.where(low_half, ys[0], ys[1])
        if d_ref is not None:
            y = y + xs_f32 * d_ref[:, cols]
        y_ref[:, cols] = y
        new_state = jnp.dot(jnp.concatenate(wbs, axis=1), xs_split, preferred_element_type=F32)
        state_ref[group * GROUP_HEADS // 2 + pair] = (
            state * jnp.where(low_half, ets[0], ets[1]) + new_state)


def _ssd_kernel(*refs, groups_per_step):
    fwd_in, bwd_in = refs[0:8], refs[8:16]
    d_ref, yf_ref, yb_ref, state_ref = refs[16:20]

    @pl.when(pl.program_id(2) == 0)
    def _():
        state_ref[...] = jnp.zeros_like(state_ref)

    for group in range(groups_per_step):
        head_base = (pl.program_id(1) * groups_per_step + group) * GROUP_HEADS
        _ssd_direction(False, group, head_base, *fwd_in, d_ref, yf_ref, state_ref.at[0])
        _ssd_direction(True, group, DT_BWD + head_base, *bwd_in, None, yb_ref, state_ref.at[1])


def _ssd(xbc, cum, cum_t, dt_t, w_t, etot, d_exp, batch, lp, cfg):
    rows = xbc.shape[0]
    nc = lp // CHUNK
    gps = min(SSD_GROUPS_PER_STEP, cfg.ssm_groups)
    assert cfg.ssm_groups % gps == 0 and (cfg.ssm_w // SSM_STATE) % gps == 0
    gw = gps * GROUP_HEADS * SSM_HEADDIM
    sw = gps * SSM_STATE
    bm_b = cfg.ssm_w // sw
    cm_b = bm_b + cfg.ssm_groups // gps

    def specs(chunk_of):
        r = lambda b, c: b * nc + chunk_of(c)
        return [pl.BlockSpec((CHUNK, gw), lambda b, g, c: (r(b, c), g)),
                pl.BlockSpec((CHUNK, sw), lambda b, g, c: (r(b, c), bm_b + g)),
                pl.BlockSpec((CHUNK, sw), lambda b, g, c: (r(b, c), cm_b + g)),
                pl.BlockSpec((CHUNK, DT_LANES), lambda b, g, c: (r(b, c), 0)),
                pl.BlockSpec((DT_LANES, CHUNK), lambda b, g, c: (0, r(b, c))),
                pl.BlockSpec((DT_LANES, CHUNK), lambda b, g, c: (0, r(b, c))),
                pl.BlockSpec((DT_LANES, CHUNK), lambda b, g, c: (0, r(b, c))),
                pl.BlockSpec((CHUNK, DT_LANES), lambda b, g, c: (r(b, c), 0))]

    fwd_chunk = lambda c: c
    bwd_chunk = lambda c: nc - 1 - c
    args = (xbc, xbc, xbc, cum, cum_t, dt_t, w_t, etot)
    return pl.pallas_call(
        functools.partial(_ssd_kernel, groups_per_step=gps),
        out_shape=(jax.ShapeDtypeStruct((rows, cfg.ssm_w), F32),
                   jax.ShapeDtypeStruct((rows, cfg.ssm_w), F32)),
        grid=(batch, cfg.ssm_groups // gps, nc),
        in_specs=specs(fwd_chunk) + specs(bwd_chunk) + [
            pl.BlockSpec((1, gw), lambda b, g, c: (0, g))],
        out_specs=(pl.BlockSpec((CHUNK, gw), lambda b, g, c: (b * nc + c, g)),
                   pl.BlockSpec((CHUNK, gw), lambda b, g, c: (b * nc + nc - 1 - c, g))),
        scratch_shapes=[pltpu.VMEM((2, gps * GROUP_HEADS // 2, SSM_STATE, LANE), F32)],
        compiler_params=_params(3),
        name="ssd_scan",
    )(*args, *args, d_exp)


def _ssd_finish_kernel(yf_ref, yb_ref, z_ref, g_ref, o_ref, *, row_chunk):
    rows, width = o_ref.shape

    def body(r, carry):
        sl = pl.ds(pl.multiple_of(r * row_chunk, row_chunk), row_chunk)
        z = z_ref[sl, :]
        y = (yf_ref[sl, :] + yb_ref[sl, :]) * (z * jax.nn.sigmoid(z))
        ms = jnp.sum(y * y, axis=-1, keepdims=True) / width
        o_ref[sl, :] = (y * lax.rsqrt(ms + NORM_EPS) * g_ref[...]).astype(BF16)
        return carry

    lax.fori_loop(0, rows // row_chunk, body, 0)


def _ssd_finish(yf, yb, proj, gain, cfg):
    rows = yf.shape[0]
    tm = _pick(rows, 256, LANE)
    w = cfg.ssm_w
    row_spec = pl.BlockSpec((tm, w), lambda i: (i, 0))
    vec_spec = pl.BlockSpec((1, w), lambda i: (0, 0))
    return pl.pallas_call(
        functools.partial(_ssd_finish_kernel, row_chunk=32),
        out_shape=jax.ShapeDtypeStruct((rows, w), BF16),
        grid=(rows // tm,),
        in_specs=[row_spec, row_spec, row_spec, vec_spec],
        out_specs=row_spec,
        compiler_params=_params(1),
        name="ssd_finish",
    )(yf, yb, proj, gain)


def _out_proj_kernel(oa_ref, os_ref, wa_ref, ws_ref, h_ref, o_ref):
    acc = jnp.dot(oa_ref[...], wa_ref[...], preferred_element_type=F32)
    acc = acc + jnp.dot(os_ref[...], ws_ref[...], preferred_element_type=F32)
    o_ref[...] = h_ref[...] + acc


def _out_proj(o_att, o_ssm, w, h, cfg):
    rows, d = h.shape
    tm = _pick(rows, 512, LANE)
    tn = _pick(d, 512, LANE)
    kw = cfg.att_w
    return pl.pallas_call(
        _out_proj_kernel,
        out_shape=jax.ShapeDtypeStruct((rows, d), F32),
        grid=(rows // tm, d // tn),
        in_specs=[pl.BlockSpec((tm, kw), lambda i, j: (i, 0)),
                  pl.BlockSpec((tm, kw), lambda i, j: (i, 0)),
                  pl.BlockSpec((kw, tn), lambda i, j: (0, j)),
                  pl.BlockSpec((kw, tn), lambda i, j: (1, j)),
                  pl.BlockSpec((tm, tn), lambda i, j: (i, j))],
        out_specs=pl.BlockSpec((tm, tn), lambda i, j: (i, j)),
        input_output_aliases={4: 0},
        compiler_params=_params(2),
        name="out_proj",
    )(o_att, o_ssm, w, w, h)


def _final_norm_kernel(x_ref, g_ref, o_ref, *, row_chunk):
    rows, width = o_ref.shape

    def body(r, carry):
        sl = pl.ds(pl.multiple_of(r * row_chunk, row_chunk), row_chunk)
        x = x_ref[sl, :]
        ms = jnp.sum(x * x, axis=-1, keepdims=True) / width
        o_ref[sl, :] = x * lax.rsqrt(ms + NORM_EPS) * g_ref[...]
        return carry

    lax.fori_loop(0, rows // row_chunk, body, 0)


def _final_norm(h, gain, batch, lp):
    d = h.shape[1]
    nc = lp // CHUNK
    seq = lp - CHUNK
    return pl.pallas_call(
        functools.partial(_final_norm_kernel, row_chunk=32),
        out_shape=jax.ShapeDtypeStruct((batch * seq, d), F32),
        grid=(batch, nc - 1),
        in_specs=[pl.BlockSpec((CHUNK, d), lambda b, c: (b * nc + 1 + c, 0)),
                  pl.BlockSpec((1, d), lambda b, c: (0, 0))],
        out_specs=pl.BlockSpec((CHUNK, d), lambda b, c: (b * (nc - 1) + c, 0)),
        compiler_params=_params(2),
        name="final_norm",
    )(h, gain)


def _prep_layer_weights(cfg, w_in, w_q_up, w_kv_up, w_out, dt_bias, a_log, d_skip):
    q, kv, hs = cfg.q_lora, cfg.kv_lora, cfg.ssm_heads
    d = cfg.d_model
    o = 0
    w_q = w_in[:, o:o + q]; o += q
    w_kv = w_in[:, o:o + kv]; o += kv
    w_kr = w_in[:, o:o + QK_ROPE]; o += QK_ROPE
    w_gate = w_in[:, o:o + cfg.att_w]; o += cfg.att_w
    w_z = w_in[:, o:o + cfg.ssm_w]; o += cfg.ssm_w
    w_xbc = w_in[:, o:o + cfg.xbc_w]; o += cfg.xbc_w
    w_dt = w_in[:, o:o + 2 * hs]

    def pad_dirs(t):
        z = jnp.zeros(t.shape[:-1] + (DT_BWD - hs,), t.dtype)
        return jnp.concatenate([t[..., :hs], z, t[..., hs:], z], axis=-1)

    w_in_p = jnp.concatenate(
        [w_z, w_gate, w_xbc, w_kv, w_q, pad_dirs(w_dt), w_kr,
         jnp.zeros((d, LANE - QK_ROPE), w_in.dtype)], axis=1).astype(BF16)

    wq = w_q_up.reshape(q, cfg.att_heads // 2, 2, QK_NOPE + QK_ROPE)
    wq_p = jnp.concatenate(
        [wq[:, :, 0, :QK_NOPE], wq[:, :, 1, :QK_NOPE], wq[:, :, 0, QK_NOPE:], wq[:, :, 1, QK_NOPE:]],
        axis=-1).reshape(q, cfg.q_w).astype(BF16)

    wkv = w_kv_up.reshape(kv, cfg.att_heads, QK_NOPE + V_DIM)
    wk = wkv[:, :, :QK_NOPE].reshape(kv, cfg.att_heads * QK_NOPE).astype(BF16)
    wv_t = wkv[:, :, QK_NOPE:].reshape(kv, cfg.att_w).T.astype(BF16)

    return dict(
        w_in=w_in_p, w_q=wq_p, wk=wk, wv_t=wv_t, w_out=w_out.astype(BF16),
        dt_bias=pad_dirs(dt_bias.reshape(1, 2 * hs)),
        a_log=pad_dirs(a_log.reshape(1, 2 * hs)),
        d_exp=jnp.repeat(d_skip, SSM_HEADDIM).reshape(1, cfg.ssm_w))


def _run_trunk(x, meta_tokens, layers, final_norm, nq, cfg):
    batch, seq, d = x.shape
    lp = seq + CHUNK
    h = jnp.concatenate(
        [jnp.zeros((batch, PAD, d), x.dtype),
         jnp.broadcast_to(meta_tokens[None].astype(x.dtype), (batch, N_META, d)), x],
        axis=1).reshape(batch * lp, d)
    cos_t, sin_t = _rope_tables(batch, lp)
    for lw in layers:
        proj = _in_proj(h, lw["norm_mix"], lw["w_in"], cfg)
        q = _q_up(proj, lw["q_norm"], lw["w_q"], cos_t, sin_t, cfg)
        kr2 = _k_rope(proj, cos_t, sin_t, cfg)
        k, vt = _kv_up(proj, lw["kv_norm"], lw["wk"], lw["wv_t"], kr2, cfg)
        o_att = _attention(q, k, vt, proj, batch, lp, nq, cfg)
        xbc = _conv(proj, lw["conv_w"], lw["conv_b"], cfg)
        cum, cum_t, dt_t, w_t, etot = _ssd_prep(proj, lw["dt_bias"], lw["a_log"], batch, lp, cfg)
        yf, yb = _ssd(xbc, cum, cum_t, dt_t, w_t, etot, lw["d_exp"], batch, lp, cfg)
        o_ssm = _ssd_finish(yf, yb, proj, lw["ssm_norm"], cfg)
        h = _out_proj(o_att, o_ssm, lw["w_out"], h, cfg)
    y = _final_norm(h, final_norm.reshape(1, d), batch, lp)
    return y.reshape(batch, seq, d)


def _query_blocks(lp):
    for nq in range(1, lp // LANE + 1):
        if lp % nq == 0 and (lp // nq) % LANE == 0 and lp // nq <= ATT_MAX_QUERIES:
            return nq
    raise ValueError(lp)


def _forward(cfg, x_prompt, x_sample, meta_tokens, norm_mix, w_in, q_norm, w_q_up, kv_norm,
             w_kv_up, conv_w, conv_b, dt_bias, a_log, d_skip, ssm_norm, w_out, final_norm):
    _check_cfg(cfg)
    depth = w_in.shape[0]
    layers = []
    for i in range(depth):
        lw = _prep_layer_weights(cfg, w_in[i], w_q_up[i], w_kv_up[i], w_out[i],
                                 dt_bias[i], a_log[i], d_skip[i])
        lw.update(norm_mix=norm_mix[i].reshape(1, -1), q_norm=q_norm[i].reshape(1, -1),
                  kv_norm=kv_norm[i].reshape(1, -1), conv_w=conv_w[i],
                  conv_b=conv_b[i].reshape(1, -1), ssm_norm=ssm_norm[i].reshape(1, -1))
        layers.append(lw)
    outs = []
    for x in (x_prompt, x_sample):
        nq = _query_blocks(x.shape[1] + CHUNK)
        outs.append(_run_trunk(x, meta_tokens, layers, final_norm, nq, cfg))
    return tuple(outs)


def kernel(x_prompt, x_sample, meta_tokens, norm_mix, w_in, q_norm, w_q_up, kv_norm, w_kv_up,
           conv_w, conv_b, dt_bias, a_log, d_skip, ssm_norm, w_out, final_norm):
    return _forward(Cfg(), x_prompt, x_sample, meta_tokens, norm_mix, w_in, q_norm, w_q_up,
                    kv_norm, w_kv_up, conv_w, conv_b, dt_bias, a_log, d_skip, ssm_norm, w_out,
                    final_norm)
```

```python
import functools
import math
from typing import NamedTuple

import jax
import jax.numpy as jnp
from jax import lax
from jax.experimental import pallas as pl
from jax.experimental.pallas import tpu as pltpu

F32 = jnp.float32
BF16 = jnp.bfloat16

N_META = 16
NORM_EPS = 1e-6
QK_NOPE = 128
QK_ROPE = 64
V_DIM = 128
ROPE_THETA = 10000.0
SSM_HEADDIM = 64
SSM_STATE = 128
CONV_WIDTH = 5
CONV_HALF = CONV_WIDTH // 2
CHUNK = 128
PAD = CHUNK - N_META
GROUP_HEADS = 8
DT_LANES = 128
DT_BWD = 64
LANE = 128
SUBLANE = 8
MXU_COLS = 256
LOG2_E = math.log2(math.e)
Q_PRESCALE = LOG2_E / math.sqrt(QK_NOPE + QK_ROPE)
VMEM_LIMIT = 56 * 1024 * 1024
ATT_KEY_CHUNK = 512
ATT_MAX_QUERIES = 2304
ONES_ROWS = 16
SSD_GROUPS_PER_STEP = 8


class Cfg(NamedTuple):
    d_model: int = 4096
    att_heads: int = 32
    q_lora: int = 1536
    kv_lora: int = 512
    ssm_heads: int = 64
    ssm_groups: int = 8

    @property
    def att_w(self):
        return self.att_heads * V_DIM

    @property
    def ssm_w(self):
        return self.ssm_heads * SSM_HEADDIM

    @property
    def gn(self):
        return self.ssm_groups * SSM_STATE

    @property
    def xbc_w(self):
        return self.ssm_w + 2 * self.gn

    @property
    def q_piece(self):
        return self.q_lora // 3

    @property
    def off_z(self):
        return 0

    @property
    def off_gate(self):
        return self.ssm_w

    @property
    def off_xbc(self):
        return self.off_gate + self.att_w

    @property
    def off_kv(self):
        return self.off_xbc + self.xbc_w

    @property
    def off_q(self):
        return self.off_kv + self.kv_lora

    @property
    def off_dt(self):
        return self.off_q + self.q_lora

    @property
    def off_kr(self):
        return self.off_dt + DT_LANES

    @property
    def proj_w(self):
        return self.off_kr + LANE

    @property
    def q_w(self):
        return (self.att_heads // 2) * 384


def _check_cfg(cfg):
    assert cfg.ssm_w == cfg.d_model == cfg.att_w
    assert cfg.ssm_heads == cfg.ssm_groups * GROUP_HEADS and cfg.ssm_heads <= DT_BWD
    assert cfg.att_heads % 2 == 0 and cfg.q_lora % 3 == 0 and cfg.q_piece % LANE == 0
    assert cfg.kv_lora % LANE == 0
    assert cfg.off_kv % cfg.kv_lora == 0 and cfg.off_q % cfg.q_piece == 0


def _pick(n, target, unit):
    best = None
    for t in range(unit, min(n, target) + 1, unit):
        if n % t == 0:
            best = t
    assert best is not None, (n, target, unit)
    return best


def _params(n_axes):
    return pltpu.CompilerParams(dimension_semantics=("arbitrary",) * n_axes,
                                vmem_limit_bytes=VMEM_LIMIT)


def _rmsnorm_rows(dst_ref, piece_refs, gain_refs, rows, row_chunk=64):
    width = sum(p.shape[1] for p in piece_refs)
    row_chunk = min(row_chunk, rows)

    def body(r, carry):
        sl = pl.ds(pl.multiple_of(r * row_chunk, row_chunk), row_chunk)
        xs = [p[sl, :] for p in piece_refs]
        ss = xs[0] * xs[0]
        ss = jnp.sum(ss, axis=-1, keepdims=True)
        for x in xs[1:]:
            ss = ss + jnp.sum(x * x, axis=-1, keepdims=True)
        rstd = lax.rsqrt(ss / width + NORM_EPS)
        off = 0
        for x, g in zip(xs, gain_refs):
            w = x.shape[1]
            dst_ref[sl, off:off + w] = (x * rstd * g[...]).astype(BF16)
            off += w
        return carry

    lax.fori_loop(0, rows // row_chunk, body, 0)


def _in_proj_kernel(x_ref, g_ref, w_ref, o_ref, u_ref):
    @pl.when(pl.program_id(1) == 0)
    def _():
        _rmsnorm_rows(u_ref, [x_ref], [g_ref], x_ref.shape[0])

    o_ref[...] = jnp.dot(u_ref[...], w_ref[...], preferred_element_type=F32)


def _in_proj(h, gain, w, cfg):
    rows, d = h.shape
    n = w.shape[1]
    tm = _pick(rows, 512, LANE)
    tn = _pick(n, 1280, MXU_COLS)
    return pl.pallas_call(
        _in_proj_kernel,
        out_shape=jax.ShapeDtypeStruct((rows, n), F32),
        grid=(rows // tm, n // tn),
        in_specs=[pl.BlockSpec((tm, d), lambda i, j: (i, 0)),
                  pl.BlockSpec((1, d), lambda i, j: (0, 0)),
                  pl.BlockSpec((d, tn), lambda i, j: (0, j))],
        out_specs=pl.BlockSpec((tm, tn), lambda i, j: (i, j)),
        scratch_shapes=[pltpu.VMEM((tm, d), BF16)],
        compiler_params=_params(2),
        name="in_proj",
    )(h, gain, w)


def _rope128(r, cos_t, sin_t):
    lane = lax.broadcasted_iota(jnp.int32, (1, LANE), 1)
    first_half = (lane % QK_ROPE) < (QK_ROPE // 2)
    partner = jnp.where(first_half,
                        pltpu.roll(r, LANE - QK_ROPE // 2, 1),
                        pltpu.roll(r, QK_ROPE // 2, 1))
    return r * cos_t + partner * sin_t


def _rope_tables(batch, lp):
    half = QK_ROPE // 2
    inv = 1.0 / (ROPE_THETA ** (jnp.arange(half, dtype=F32) / half))
    pos = jnp.maximum(jnp.arange(lp, dtype=F32) - PAD, 0.0)
    ang = pos[:, None] * inv[None, :]
    cos, sin = jnp.cos(ang), jnp.sin(ang)
    cos_t = jnp.tile(cos, (batch, LANE // half))
    sin_t = jnp.tile(jnp.concatenate([-sin, sin], axis=1), (batch, LANE // QK_ROPE))
    return cos_t, sin_t


def _q_up_kernel(x0_ref, x1_ref, x2_ref, g0_ref, g1_ref, g2_ref, w_ref, cos_ref, sin_ref,
                 o_ref, u_ref):
    @pl.when(pl.program_id(1) == 0)
    def _():
        _rmsnorm_rows(u_ref, [x0_ref, x1_ref, x2_ref], [g0_ref, g1_ref, g2_ref],
                      x0_ref.shape[0])

    acc = jnp.dot(u_ref[...], w_ref[...], preferred_element_type=F32) * Q_PRESCALE
    pair_w = 2 * QK_NOPE + LANE
    for p in range(acc.shape[1] // pair_w):
        base = p * pair_w
        o_ref[:, base:base + 2 * QK_NOPE] = acc[:, base:base + 2 * QK_NOPE].astype(BF16)
        r = acc[:, base + 2 * QK_NOPE:base + pair_w]
        o_ref[:, base + 2 * QK_NOPE:base + pair_w] = _rope128(
            r, cos_ref[...], sin_ref[...]).astype(BF16)


def _q_up(proj, gain, w, cos_t, sin_t, cfg):
    rows = proj.shape[0]
    tm = _pick(rows, 512, LANE)
    qp = cfg.q_piece
    tn = _pick(cfg.q_w, 1536, 768)
    pb = cfg.off_q // qp
    x_specs = [pl.BlockSpec((tm, qp), functools.partial(lambda i, j, k: (i, pb + k), k=k))
               for k in range(3)]
    g_specs = [pl.BlockSpec((1, qp), functools.partial(lambda i, j, k: (0, k), k=k))
               for k in range(3)]
    return pl.pallas_call(
        _q_up_kernel,
        out_shape=jax.ShapeDtypeStruct((rows, cfg.q_w), BF16),
        grid=(rows // tm, cfg.q_w // tn),
        in_specs=x_specs + g_specs + [
            pl.BlockSpec((cfg.q_lora, tn), lambda i, j: (0, j)),
            pl.BlockSpec((tm, LANE), lambda i, j: (i, 0)),
            pl.BlockSpec((tm, LANE), lambda i, j: (i, 0))],
        out_specs=pl.BlockSpec((tm, tn), lambda i, j: (i, j)),
        scratch_shapes=[pltpu.VMEM((tm, cfg.q_lora), BF16)],
        compiler_params=_params(2),
        name="q_up",
    )(proj, proj, proj, gain, gain, gain, w, cos_t, sin_t)


def _k_rope_kernel(x_ref, cos_ref, sin_ref, o_ref):
    r = _rope128(x_ref[...], cos_ref[...], sin_ref[...])
    o_ref[...] = (r + pltpu.roll(r, QK_ROPE, 1)).astype(BF16)


def _k_rope(proj, cos_t, sin_t, cfg):
    rows = proj.shape[0]
    tm = _pick(rows, 512, LANE)
    cb = cfg.off_kr // LANE
    return pl.pallas_call(
        _k_rope_kernel,
        out_shape=jax.ShapeDtypeStruct((rows, LANE), BF16),
        grid=(rows // tm,),
        in_specs=[pl.BlockSpec((tm, LANE), lambda i: (i, cb)),
                  pl.BlockSpec((tm, LANE), lambda i: (i, 0)),
                  pl.BlockSpec((tm, LANE), lambda i: (i, 0))],
        out_specs=pl.BlockSpec((tm, LANE), lambda i: (i, 0)),
        compiler_params=_params(1),
        name="k_rope",
    )(proj, cos_t, sin_t)


def _kv_up_kernel(x_ref, g_ref, wk_ref, wvt_ref, kr_ref, k_ref, vt_ref, u_ref):
    @pl.when(pl.program_id(1) == 0)
    def _():
        _rmsnorm_rows(u_ref, [x_ref], [g_ref], x_ref.shape[0])

    u = u_ref[...]
    kn = jnp.dot(u, wk_ref[...], preferred_element_type=F32)
    kr2 = kr_ref[...]
    slot_a_lanes = lax.broadcasted_iota(jnp.int32, (1, LANE), 1) < QK_ROPE
    zero = jnp.zeros_like(kr2)
    rope_cols = (jnp.where(slot_a_lanes, kr2, zero), jnp.where(slot_a_lanes, zero, kr2))
    heads = wk_ref.shape[1] // QK_NOPE
    for t in range(heads):
        base = t * 2 * QK_NOPE
        k_ref[:, base:base + QK_NOPE] = kn[:, t * QK_NOPE:(t + 1) * QK_NOPE].astype(BF16)
        k_ref[:, base + QK_NOPE:base + 2 * QK_NOPE] = rope_cols[t % 2]
    vt_ref[...] = lax.dot_general(wvt_ref[...], u, (((1,), (1,)), ((), ())),
                                  preferred_element_type=F32).astype(BF16)


def _kv_up(proj, gain, wk, wv_t, kr2, cfg):
    rows = proj.shape[0]
    tm = _pick(rows, 512, LANE)
    hb = 8 if cfg.att_heads % 8 == 0 else 2
    kvb = cfg.off_kv // cfg.kv_lora
    return pl.pallas_call(
        _kv_up_kernel,
        out_shape=(jax.ShapeDtypeStruct((rows, cfg.att_heads * 2 * QK_NOPE), BF16),
                   jax.ShapeDtypeStruct((cfg.att_w, rows), BF16)),
        grid=(rows // tm, cfg.att_heads // hb),
        in_specs=[pl.BlockSpec((tm, cfg.kv_lora), lambda i, j: (i, kvb)),
                  pl.BlockSpec((1, cfg.kv_lora), lambda i, j: (0, 0)),
                  pl.BlockSpec((cfg.kv_lora, hb * QK_NOPE), lambda i, j: (0, j)),
                  pl.BlockSpec((hb * V_DIM, cfg.kv_lora), lambda i, j: (j, 0)),
                  pl.BlockSpec((tm, LANE), lambda i, j: (i, 0))],
        out_specs=(pl.BlockSpec((tm, hb * 2 * QK_NOPE), lambda i, j: (i, j)),
                   pl.BlockSpec((hb * V_DIM, tm), lambda i, j: (j, i))),
        scratch_shapes=[pltpu.VMEM((tm, cfg.kv_lora), BF16)],
        compiler_params=_params(2),
        name="kv_up",
    )(proj, gain, wk, wv_t, kr2)


def _attention_kernel(q_ref, k_ref, vt_ref, gate_ref, o_ref, *, key_chunk):
    lp = k_ref.shape[0]
    q_rope = q_ref[:, 2 * QK_NOPE:]
    nt_dims = (((1,), (1,)), ((), ()))
    for t in range(2):
        hcols = slice(t * V_DIM, (t + 1) * V_DIM)
        kcols = slice(t * 2 * QK_NOPE, (t + 1) * 2 * QK_NOPE)
        q = jnp.concatenate([q_ref[:, hcols], q_rope], axis=1)

        def weighted_values(s_t, m_new, a, size):
            p_t = jnp.exp2(s_t - m_new).astype(BF16)
            vt_ones = jnp.concatenate(
                [vt_ref[hcols, a:a + size], jnp.ones((ONES_ROWS, size), BF16)], axis=0)
            return jnp.dot(vt_ones, p_t, preferred_element_type=F32)

        s_t = lax.dot_general(k_ref[0:CHUNK, kcols], q, nt_dims, preferred_element_type=F32)
        key = lax.broadcasted_iota(jnp.int32, (CHUNK, 1), 0)
        s_t = jnp.where(key >= PAD, s_t, -jnp.inf)
        m = jnp.max(s_t, axis=0, keepdims=True)
        acc = weighted_values(s_t, m, 0, CHUNK)

        for a in range(CHUNK, lp, key_chunk):
            s_t = lax.dot_general(k_ref[a:a + key_chunk, kcols], q, nt_dims,
                                  preferred_element_type=F32)
            m_new = jnp.maximum(m, jnp.max(s_t, axis=0, keepdims=True))
            acc = jnp.exp2(m - m_new) * acc + weighted_values(s_t, m_new, a, key_chunk)
            m = m_new

        out = (acc[:V_DIM, :] / acc[V_DIM:V_DIM + 1, :]).T
        g = gate_ref[:, hcols]
        o_ref[:, hcols] = (out * (g * jax.nn.sigmoid(g))).astype(BF16)


def _attention(q, k, vt, proj, batch, lp, nq, cfg):
    rows = q.shape[0]
    tq = lp // nq
    assert lp % nq == 0 and tq % LANE == 0
    key_chunk = _pick(lp - CHUNK, ATT_KEY_CHUNK, LANE)
    pair_v = 2 * V_DIM
    gate_b = cfg.off_gate // pair_v
    kern = functools.partial(_attention_kernel, key_chunk=key_chunk)
    return pl.pallas_call(
        kern,
        out_shape=jax.ShapeDtypeStruct((rows, cfg.att_w), BF16),
        grid=(batch, cfg.att_heads // 2, nq),
        in_specs=[
            pl.BlockSpec((tq, 2 * QK_NOPE + LANE), lambda b, h, i: (b * nq + i, h)),
            pl.BlockSpec((lp, 4 * QK_NOPE), lambda b, h, i: (b, h)),
            pl.BlockSpec((pair_v, lp), lambda b, h, i: (h, b)),
            pl.BlockSpec((tq, pair_v), lambda b, h, i: (b * nq + i, gate_b + h))],
        out_specs=pl.BlockSpec((tq, pair_v), lambda b, h, i: (b * nq + i, h)),
        compiler_params=_params(3),
        name="attention",
    )(q, k, vt, proj)


def _conv_kernel(x_ref, prev_ref, next_ref, w_ref, b_ref, o_ref, ext_ref):
    i = pl.program_id(0)
    tm = x_ref.shape[0]
    ext_ref[0:SUBLANE, :] = jnp.where(i == 0, 0.0, prev_ref[...])
    ext_ref[SUBLANE:SUBLANE + tm, :] = x_ref[...]
    ext_ref[SUBLANE + tm:2 * SUBLANE + tm, :] = jnp.where(
        i == pl.num_programs(0) - 1, 0.0, next_ref[...])
    ext = ext_ref[...]
    n = ext.shape[0]
    acc = b_ref[...]
    for tap in range(CONV_WIDTH):
        shifted = ext if tap == CONV_HALF else pltpu.roll(ext, (CONV_HALF - tap) % n, 0)
        acc = acc + shifted[SUBLANE:SUBLANE + tm, :] * w_ref[tap:tap + 1, :]
    o_ref[...] = acc * jax.nn.sigmoid(acc)


def _conv(proj, w, b, cfg):
    rows = proj.shape[0]
    tm = _pick(rows, 512, LANE)
    tc = _pick(math.gcd(cfg.xbc_w, cfg.off_xbc), 2048, LANE)
    cb = cfg.off_xbc // tc
    hb = tm // SUBLANE
    last = rows // SUBLANE - 1
    return pl.pallas_call(
        _conv_kernel,
        out_shape=jax.ShapeDtypeStruct((rows, cfg.xbc_w), F32),
        grid=(rows // tm, cfg.xbc_w // tc),
        in_specs=[pl.BlockSpec((tm, tc), lambda i, j: (i, cb + j)),
                  pl.BlockSpec((SUBLANE, tc), lambda i, j: (jnp.maximum(i * hb - 1, 0), cb + j)),
                  pl.BlockSpec((SUBLANE, tc),
                               lambda i, j: (jnp.minimum((i + 1) * hb, last), cb + j)),
                  pl.BlockSpec((CONV_WIDTH, tc), lambda i, j: (0, j)),
                  pl.BlockSpec((1, tc), lambda i, j: (0, j))],
        out_specs=pl.BlockSpec((tm, tc), lambda i, j: (i, j)),
        scratch_shapes=[pltpu.VMEM((tm + 2 * SUBLANE, tc), F32)],
        compiler_params=_params(2),
        name="conv_silu",
    )(proj, proj, proj, w, b)


def _ssd_prep_kernel(raw_ref, bias_ref, alog_ref, cum_ref, cumt_ref, dtt_ref, wt_ref, etot_ref):
    x = raw_ref[...] + bias_ref[...]
    dt = jnp.maximum(x, 0.0) + jnp.log1p(jnp.exp(-jnp.abs(x)))
    row = lax.broadcasted_iota(jnp.int32, (CHUNK, DT_LANES), 0)
    lane = lax.broadcasted_iota(jnp.int32, (CHUNK, DT_LANES), 1)
    dt = jnp.where(jnp.logical_and(pl.program_id(1) == 0, row < PAD), 0.0, dt)
    da = dt * (-jnp.exp(alog_ref[...]))
    fwd = da
    bwd = da
    k = 1
    while k < CHUNK:
        fwd = fwd + jnp.where(row >= k, pltpu.roll(fwd, k, 0), 0.0)
        bwd = bwd + jnp.where(row < CHUNK - k, pltpu.roll(bwd, CHUNK - k, 0), 0.0)
        k *= 2
    is_fwd = lane < DT_BWD
    cum = jnp.where(is_fwd, fwd, bwd)
    tot = jnp.where(is_fwd[0:1, :], fwd[CHUNK - 1:CHUNK, :], bwd[0:1, :])
    w = jnp.exp(tot - cum) * dt
    cum = cum * LOG2_E
    cum_ref[...] = cum
    cumt_ref[...] = cum.T
    dtt_ref[...] = dt.T
    wt_ref[...] = w.T
    etot_ref[...] = jnp.broadcast_to(jnp.exp(tot), (CHUNK, DT_LANES)).T


def _ssd_prep(proj, bias, alog, batch, lp, cfg):
    rows = proj.shape[0]
    nc = lp // CHUNK
    cb = cfg.off_dt // DT_LANES
    row_major = pl.BlockSpec((CHUNK, DT_LANES), lambda b, c: (b * nc + c, 0))
    col_major = pl.BlockSpec((DT_LANES, CHUNK), lambda b, c: (0, b * nc + c))
    return pl.pallas_call(
        _ssd_prep_kernel,
        out_shape=(jax.ShapeDtypeStruct((rows, DT_LANES), F32),
                   jax.ShapeDtypeStruct((DT_LANES, rows), F32),
                   jax.ShapeDtypeStruct((DT_LANES, rows), F32),
                   jax.ShapeDtypeStruct((DT_LANES, rows), F32),
                   jax.ShapeDtypeStruct((rows, DT_LANES), F32)),
        grid=(batch, nc),
        in_specs=[pl.BlockSpec((CHUNK, DT_LANES), lambda b, c: (b * nc + c, cb)),
                  pl.BlockSpec((1, DT_LANES), lambda b, c: (0, 0)),
                  pl.BlockSpec((1, DT_LANES), lambda b, c: (0, 0))],
        out_specs=(row_major, col_major, col_major, col_major, row_major),
        compiler_params=_params(2),
        name="ssd_prep",
    )(proj, bias, alog)


def _ssd_direction(reverse, group, head_base, xs_ref, bm_ref, cm_ref, cum_ref, cumt_ref,
                   dtt_ref, wt_ref, etot_ref, d_ref, y_ref, state_ref):
    gcols = slice(group * SSM_STATE, (group + 1) * SSM_STATE)
    bm = bm_ref[:, gcols]
    cm = cm_ref[:, gcols]
    bm_t = bm.T.astype(BF16)
    cm = cm.astype(BF16)
    cb = jnp.dot(cm, bm_t, preferred_element_type=F32).astype(BF16)
    row = lax.broadcasted_iota(jnp.int32, (CHUNK, CHUNK), 0)
    col = lax.broadcasted_iota(jnp.int32, (CHUNK, CHUNK), 1)
    live = (col >= row) if reverse else (row >= col)
    lane = lax.broadcasted_iota(jnp.int32, (CHUNK, DT_LANES), 1)
    low_half = lax.broadcasted_iota(jnp.int32, (1, LANE), 1) < SSM_HEADDIM
    cum = cum_ref[...]
    for pair in range(GROUP_HEADS // 2):
        cols = slice((group * GROUP_HEADS // 2 + pair) * LANE,
                     (group * GROUP_HEADS // 2 + pair + 1) * LANE)
        xs_f32 = xs_ref[:, cols]
        xs_pair = xs_f32.astype(BF16)
        state = state_ref[group * GROUP_HEADS // 2 + pair]
        rhs = jnp.concatenate([xs_pair, state.astype(BF16)], axis=0)
        xs_split = jnp.concatenate([jnp.where(low_half, xs_pair, jnp.zeros_like(xs_pair)),
                                    jnp.where(low_half, jnp.zeros_like(xs_pair), xs_pair)], axis=0)
        ys, wbs, ets = [], [], []
        for t in range(2):
            hd = head_base + 2 * pair + t
            a_col = jnp.sum(jnp.where(lane == hd, cum, 0.0), axis=1, keepdims=True)
            a_row = cumt_ref[pl.ds(hd, 1), :]
            decay = jnp.exp2(jnp.where(live, a_col - a_row, -jnp.inf)).astype(BF16)
            mix = cb * decay * dtt_ref[pl.ds(hd, 1), :].astype(BF16)
            carry_in = cm * jnp.broadcast_to(jnp.exp2(a_col), (CHUNK, SSM_STATE)).astype(BF16)
            lhs = jnp.concatenate([mix, carry_in], axis=1)
            ys.append(jnp.dot(lhs, rhs, preferred_element_type=F32))
            wbs.append(bm_t * wt_ref[pl.ds(hd, 1), :].astype(BF16))
            ets.append(etot_ref[pl.ds(hd, 1), :])
        y = jnp.where(low_half, ys[0], ys[1])
        if d_ref is not None:
            y = y + xs_f32 * d_ref[:, cols]
        y_ref[:, cols] = y.astype(y_ref.dtype)
        new_state = jnp.dot(jnp.concatenate(wbs, axis=1), xs_split, preferred_element_type=F32)
        state_ref[group * GROUP_HEADS // 2 + pair] = (
            state * jnp.where(low_half, ets[0], ets[1]) + new_state)


def _ssd_kernel(*refs, groups_per_step):
    fwd_in, bwd_in = refs[0:8], refs[8:16]
    d_ref, yf_ref, yb_ref, state_ref = refs[16:20]

    @pl.when(pl.program_id(2) == 0)
    def _():
        state_ref[...] = jnp.zeros_like(state_ref)

    for group in range(groups_per_step):
        head_base = (pl.program_id(1) * groups_per_step + group) * GROUP_HEADS
        _ssd_direction(False, group, head_base, *fwd_in, d_ref, yf_ref, state_ref.at[0])
        _ssd_direction(True, group, DT_BWD + head_base, *bwd_in, None, yb_ref, state_ref.at[1])


def _ssd(xbc, cum, cum_t, dt_t, w_t, etot, d_exp, batch, lp, cfg):
    rows = xbc.shape[0]
    nc = lp // CHUNK
    gps = min(SSD_GROUPS_PER_STEP, cfg.ssm_groups)
    assert cfg.ssm_groups % gps == 0 and (cfg.ssm_w // SSM_STATE) % gps == 0
    gw = gps * GROUP_HEADS * SSM_HEADDIM
    sw = gps * SSM_STATE
    bm_b = cfg.ssm_w // sw
    cm_b = bm_b + cfg.ssm_groups // gps

    def specs(chunk_of):
        r = lambda b, c: b * nc + chunk_of(c)
        return [pl.BlockSpec((CHUNK, gw), lambda b, g, c: (r(b, c), g)),
                pl.BlockSpec((CHUNK, sw), lambda b, g, c: (r(b, c), bm_b + g)),
                pl.BlockSpec((CHUNK, sw), lambda b, g, c: (r(b, c), cm_b + g)),
                pl.BlockSpec((CHUNK, DT_LANES), lambda b, g, c: (r(b, c), 0)),
                pl.BlockSpec((DT_LANES, CHUNK), lambda b, g, c: (0, r(b, c))),
                pl.BlockSpec((DT_LANES, CHUNK), lambda b, g, c: (0, r(b, c))),
                pl.BlockSpec((DT_LANES, CHUNK), lambda b, g, c: (0, r(b, c))),
                pl.BlockSpec((CHUNK, DT_LANES), lambda b, g, c: (r(b, c), 0))]

    fwd_chunk = lambda c: c
    bwd_chunk = lambda c: nc - 1 - c
    args = (xbc, xbc, xbc, cum, cum_t, dt_t, w_t, etot)
    return pl.pallas_call(
        functools.partial(_ssd_kernel, groups_per_step=gps),
        out_shape=(jax.ShapeDtypeStruct((rows, cfg.ssm_w), BF16),
                   jax.ShapeDtypeStruct((rows, cfg.ssm_w), BF16)),
        grid=(batch, cfg.ssm_groups // gps, nc),
        in_specs=specs(fwd_chunk) + specs(bwd_chunk) + [
            pl.BlockSpec((1, gw), lambda b, g, c: (0, g))],
        out_specs=(pl.BlockSpec((CHUNK, gw), lambda b, g, c: (b * nc + c, g)),
                   pl.BlockSpec((CHUNK, gw), lambda b, g, c: (b * nc + nc - 1 - c, g))),
        scratch_shapes=[pltpu.VMEM((2, gps * GROUP_HEADS // 2, SSM_STATE, LANE), F32)],
        compiler_params=_params(3),
        name="ssd_scan",
    )(*args, *args, d_exp)


def _ssd_finish_kernel(yf_ref, yb_ref, z_ref, g_ref, o_ref, *, row_chunk):
    rows, width = o_ref.shape

    def body(r, carry):
        sl = pl.ds(pl.multiple_of(r * row_chunk, row_chunk), row_chunk)
        z = z_ref[sl, :]
        y = yf_ref[sl, :].astype(F32) + yb_ref[sl, :].astype(F32)
        y = y * (z * jax.nn.sigmoid(z))
        ms = jnp.sum(y * y, axis=-1, keepdims=True) / width
        o_ref[sl, :] = (y * lax.rsqrt(ms + NORM_EPS) * g_ref[...]).astype(BF16)
        return carry

    lax.fori_loop(0, rows // row_chunk, body, 0)


def _ssd_finish(yf, yb, proj, gain, cfg):
    rows = yf.shape[0]
    tm = _pick(rows, 256, LANE)
    w = cfg.ssm_w
    row_spec = pl.BlockSpec((tm, w), lambda i: (i, 0))
    vec_spec = pl.BlockSpec((1, w), lambda i: (0, 0))
    return pl.pallas_call(
        functools.partial(_ssd_finish_kernel, row_chunk=32),
        out_shape=jax.ShapeDtypeStruct((rows, w), BF16),
        grid=(rows // tm,),
        in_specs=[row_spec, row_spec, row_spec, vec_spec],
        out_specs=row_spec,
        compiler_params=_params(1),
        name="ssd_finish",
    )(yf, yb, proj, gain)


def _out_proj_kernel(oa_ref, os_ref, wa_ref, ws_ref, h_ref, o_ref):
    acc = jnp.dot(oa_ref[...], wa_ref[...], preferred_element_type=F32)
    acc = acc + jnp.dot(os_ref[...], ws_ref[...], preferred_element_type=F32)
    o_ref[...] = h_ref[...] + acc


def _out_proj(o_att, o_ssm, w, h, cfg):
    rows, d = h.shape
    tm = _pick(rows, 512, LANE)
    tn = _pick(d, 512, LANE)
    kw = cfg.att_w
    return pl.pallas_call(
        _out_proj_kernel,
        out_shape=jax.ShapeDtypeStruct((rows, d), F32),
        grid=(rows // tm, d // tn),
        in_specs=[pl.BlockSpec((tm, kw), lambda i, j: (i, 0)),
                  pl.BlockSpec((tm, kw), lambda i, j: (i, 0)),
                  pl.BlockSpec((kw, tn), lambda i, j: (0, j)),
                  pl.BlockSpec((kw, tn), lambda i, j: (1, j)),
                  pl.BlockSpec((tm, tn), lambda i, j: (i, j))],
        out_specs=pl.BlockSpec((tm, tn), lambda i, j: (i, j)),
        input_output_aliases={4: 0},
        compiler_params=_params(2),
        name="out_proj",
    )(o_att, o_ssm, w, w, h)


def _final_norm_kernel(x_ref, g_ref, o_ref, *, row_chunk):
    rows, width = o_ref.shape

    def body(r, carry):
        sl = pl.ds(pl.multiple_of(r * row_chunk, row_chunk), row_chunk)
        x = x_ref[sl, :]
        ms = jnp.sum(x * x, axis=-1, keepdims=True) / width
        o_ref[sl, :] = x * lax.rsqrt(ms + NORM_EPS) * g_ref[...]
        return carry

    lax.fori_loop(0, rows // row_chunk, body, 0)


def _final_norm(h, gain, batch, lp):
    d = h.shape[1]
    nc = lp // CHUNK
    seq = lp - CHUNK
    return pl.pallas_call(
        functools.partial(_final_norm_kernel, row_chunk=32),
        out_shape=jax.ShapeDtypeStruct((batch * seq, d), F32),
        grid=(batch, nc - 1),
        in_specs=[pl.BlockSpec((CHUNK, d), lambda b, c: (b * nc + 1 + c, 0)),
                  pl.BlockSpec((1, d), lambda b, c: (0, 0))],
        out_specs=pl.BlockSpec((CHUNK, d), lambda b, c: (b * (nc - 1) + c, 0)),
        compiler_params=_params(2),
        name="final_norm",
    )(h, gain)


def _prep_layer_weights(cfg, w_in, w_q_up, w_kv_up, w_out, dt_bias, a_log, d_skip):
    q, kv, hs = cfg.q_lora, cfg.kv_lora, cfg.ssm_heads
    d = cfg.d_model
    o = 0
    w_q = w_in[:, o:o + q]; o += q
    w_kv = w_in[:, o:o + kv]; o += kv
    w_kr = w_in[:, o:o + QK_ROPE]; o += QK_ROPE
    w_gate = w_in[:, o:o + cfg.att_w]; o += cfg.att_w
    w_z = w_in[:, o:o + cfg.ssm_w]; o += cfg.ssm_w
    w_xbc = w_in[:, o:o + cfg.xbc_w]; o += cfg.xbc_w
    w_dt = w_in[:, o:o + 2 * hs]

    def pad_dirs(t):
        z = jnp.zeros(t.shape[:-1] + (DT_BWD - hs,), t.dtype)
        return jnp.concatenate([t[..., :hs], z, t[..., hs:], z], axis=-1)

    w_in_p = jnp.concatenate(
        [t.astype(BF16) for t in (w_z, w_gate, w_xbc, w_kv, w_q, pad_dirs(w_dt), w_kr)]
        + [jnp.zeros((d, LANE - QK_ROPE), BF16)], axis=1)

    wq = w_q_up.reshape(q, cfg.att_heads // 2, 2, QK_NOPE + QK_ROPE)
    wq_p = jnp.concatenate(
        [wq[:, :, 0, :QK_NOPE], wq[:, :, 1, :QK_NOPE], wq[:, :, 0, QK_NOPE:], wq[:, :, 1, QK_NOPE:]],
        axis=-1).reshape(q, cfg.q_w).astype(BF16)

    wkv = w_kv_up.reshape(kv, cfg.att_heads, QK_NOPE + V_DIM)
    wk = wkv[:, :, :QK_NOPE].reshape(kv, cfg.att_heads * QK_NOPE).astype(BF16)
    wv_t = wkv[:, :, QK_NOPE:].reshape(kv, cfg.att_w).T.astype(BF16)

    return dict(
        w_in=w_in_p, w_q=wq_p, wk=wk, wv_t=wv_t, w_out=w_out.astype(BF16),
        dt_bias=pad_dirs(dt_bias.reshape(1, 2 * hs)),
        a_log=pad_dirs(a_log.reshape(1, 2 * hs)),
        d_exp=jnp.repeat(d_skip, SSM_HEADDIM).reshape(1, cfg.ssm_w))


def _run_trunk(x, meta_tokens, layers, final_norm, nq, cfg):
    batch, seq, d = x.shape
    lp = seq + CHUNK
    h = jnp.concatenate(
        [jnp.zeros((batch, PAD, d), x.dtype),
         jnp.broadcast_to(meta_tokens[None].astype(x.dtype), (batch, N_META, d)), x],
        axis=1).reshape(batch * lp, d)
    cos_t, sin_t = _rope_tables(batch, lp)
    for lw in layers:
        proj = _in_proj(h, lw["norm_mix"], lw["w_in"], cfg)
        q = _q_up(proj, lw["q_norm"], lw["w_q"], cos_t, sin_t, cfg)
        kr2 = _k_rope(proj, cos_t, sin_t, cfg)
        k, vt = _kv_up(proj, lw["kv_norm"], lw["wk"], lw["wv_t"], kr2, cfg)
        o_att = _attention(q, k, vt, proj, batch, lp, nq, cfg)
        xbc = _conv(proj, lw["conv_w"], lw["conv_b"], cfg)
        cum, cum_t, dt_t, w_t, etot = _ssd_prep(proj, lw["dt_bias"], lw["a_log"], batch, lp, cfg)
        yf, yb = _ssd(xbc, cum, cum_t, dt_t, w_t, etot, lw["d_exp"], batch, lp, cfg)
        o_ssm = _ssd_finish(yf, yb, proj, lw["ssm_norm"], cfg)
        h = _out_proj(o_att, o_ssm, lw["w_out"], h, cfg)
    y = _final_norm(h, final_norm.reshape(1, d), batch, lp)
    return y.reshape(batch, seq, d)


def _query_blocks(lp):
    for nq in range(1, lp // LANE + 1):
        if lp % nq == 0 and (lp // nq) % LANE == 0 and lp // nq <= ATT_MAX_QUERIES:
            return nq
    raise ValueError(lp)


def _forward(cfg, x_prompt, x_sample, meta_tokens, norm_mix, w_in, q_norm, w_q_up, kv_norm,
             w_kv_up, conv_w, conv_b, dt_bias, a_log, d_skip, ssm_norm, w_out, final_norm):
    _check_cfg(cfg)
    depth = w_in.shape[0]
    layers = []
    for i in range(depth):
        lw = _prep_layer_weights(cfg, w_in[i], w_q_up[i], w_kv_up[i], w_out[i],
                                 dt_bias[i], a_log[i], d_skip[i])
        lw.update(norm_mix=norm_mix[i].reshape(1, -1), q_norm=q_norm[i].reshape(1, -1),
                  kv_norm=kv_norm[i].reshape(1, -1), conv_w=conv_w[i],
                  conv_b=conv_b[i].reshape(1, -1), ssm_norm=ssm_norm[i].reshape(1, -1))
        layers.append(lw)
    outs = []
    for x in (x_prompt, x_sample):
        nq = _query_blocks(x.shape[1] + CHUNK)
        outs.append(_run_trunk(x, meta_tokens, layers, final_norm, nq, cfg))
    return tuple(outs)


def kernel(x_prompt, x_sample, meta_tokens, norm_mix, w_in, q_norm, w_q_up, kv_norm, w_kv_up,
           conv_w, conv_b, dt_bias, a_log, d_skip, ssm_norm, w_out, final_norm):
    return _forward(Cfg(), x_prompt, x_sample, meta_tokens, norm_mix, w_in, q_norm, w_q_up,
                    kv_norm, w_kv_up, conv_w, conv_b, dt_bias, a_log, d_skip, ssm_norm, w_out,
                    final_norm)
```
